```python
import math
import jax, jax.numpy as jnp
from jax import lax
import numpy as np

D_MODEL = 1024
BATCH = 16
SEQ = 256
DEPTH = 2
DEC_BATCH = 8
DEC_SEQ = 1024
PAST_LEN = 512

GRID_W = 64
D_MIX = 2 * D_MODEL
A_INNER = D_MIX // 2
A_HEAD_DIM = 64
A_HEADS = A_INNER // A_HEAD_DIM
A_GROUPS = 4
A_HPG = A_HEADS // A_GROUPS
A_D_STATE = 128
A_GN = A_GROUPS * A_D_STATE
A_CONV = 4
A_CONV_CH = A_INNER + 2 * A_GN
A_CHUNK = 128
A_IN = A_INNER + A_CONV_CH + 2 * A_HEADS
B_CH = D_MIX // 2
B_CONV = 31
B_IN = 3 * B_CH
L0_IN = A_IN + B_IN
L0_OUT = A_INNER + B_CH
C_WIDTH = D_MIX
C_GROUPS = 16
C_GROUP_DIM = C_WIDTH // C_GROUPS
C_CHUNK = 128
L1_IN = 3 * C_WIDTH
ALPHA = (2 * DEPTH) ** 0.25
BETA = (8 * DEPTH) ** -0.25
LN_EPS = 1e-5
RMS_EPS = 1e-5
POS_BASE = 10000.0

kernel_name = "hybrid_ssd_conformer_gmlp_diffusion_step"


def layer_norm(x, g, b):
    xf = x.astype(jnp.float32)
    mu = jnp.mean(xf, axis=-1, keepdims=True)
    var = jnp.mean(jnp.square(xf - mu), axis=-1, keepdims=True)
    out = (xf - mu) * lax.rsqrt(var + LN_EPS) * g.astype(jnp.float32) + b.astype(jnp.float32)
    return out.astype(x.dtype)


def depthwise_conv(x, w, bias, pad_lo, pad_hi):
    C = w.shape[1]
    out = lax.conv_general_dilated(
        x, w[:, None, :].astype(x.dtype), window_strides=(1,),
        padding=[(pad_lo, pad_hi)], dimension_numbers=("NWC", "WIO", "NWC"),
        feature_group_count=C)
    return out + bias.astype(x.dtype)


def adaln(cond, w, b):
    m = (jax.nn.silu(cond) @ w + b)[:, None, :]
    return jnp.split(m, 3, axis=-1)


def grid_pos_embed(L, d):
    rows = L // GRID_W
    quarter = d // 4
    freqs = 1.0 / (POS_BASE ** (jnp.arange(quarter, dtype=jnp.float32) / quarter))
    r = jnp.arange(rows, dtype=jnp.float32)[:, None] * freqs
    cc = jnp.arange(GRID_W, dtype=jnp.float32)[:, None] * freqs
    emb_r = jnp.concatenate([jnp.sin(r), jnp.cos(r)], axis=-1)
    emb_c = jnp.concatenate([jnp.sin(cc), jnp.cos(cc)], axis=-1)
    emb = jnp.concatenate([
        jnp.broadcast_to(emb_r[:, None, :], (rows, GRID_W, d // 2)),
        jnp.broadcast_to(emb_c[None, :, :], (rows, GRID_W, d // 2))], axis=-1)
    return emb.reshape(rows * GRID_W, d)


def ssd_chunked_scan(x, dt, a_neg, bm, cm, h0):
    b, L, G, R, P = x.shape
    N = bm.shape[-1]
    nc = L // A_CHUNK
    x = x.reshape(b, nc, A_CHUNK, G, R, P)
    dt = dt.reshape(b, nc, A_CHUNK, G, R)
    bm = bm.reshape(b, nc, A_CHUNK, G, N)
    cm = cm.reshape(b, nc, A_CHUNK, G, N)
    acs = jnp.cumsum(dt * a_neg, axis=2)
    tril = jnp.tril(jnp.ones((A_CHUNK, A_CHUNK), dtype=bool))
    seg = acs[:, :, :, None] - acs[:, :, None, :]
    lmat = jnp.exp(jnp.where(tril[:, :, None, None], seg, -jnp.inf))
    xdt = x * dt[..., None]
    cb = jnp.einsum("bcign,bcjgn->bcijg", cm, bm)
    y_diag = jnp.einsum("bcijg,bcijgr,bcjgrp->bcigrp", cb, lmat, xdt)
    decay_end = jnp.exp(acs[:, :, -1:] - acs)
    chunk_states = jnp.einsum("bcjgn,bcjgr,bcjgrp->bcgrpn", bm, decay_end, xdt)
    chunk_decay = jnp.exp(acs[:, :, -1])

    def step(h, inp):
        dec, st = inp
        return h * dec[..., None, None] + st, h

    h_final, h_in = lax.scan(step, h0, (jnp.moveaxis(chunk_decay, 1, 0), jnp.moveaxis(chunk_states, 1, 0)))
    h_in = jnp.moveaxis(h_in, 0, 1)
    y_off = jnp.einsum("bcign,bcigr,bcgrpn->bcigrp", cm, jnp.exp(acs), h_in)
    return (y_diag + y_off).reshape(b, L, G, R, P), h_final


def ssd_mixer(pa, h0f, h0b, p):
    b, L, _ = pa.shape
    f32 = jnp.float32
    z = pa[..., :A_INNER]
    xbc = pa[..., A_INNER:A_INNER + A_CONV_CH]
    dt_raw = pa[..., A_INNER + A_CONV_CH:].astype(f32)
    xbc = jax.nn.silu(depthwise_conv(xbc, p["a_conv_w"], p["a_conv_b"], A_CONV // 2, A_CONV - 1 - A_CONV // 2)).astype(f32)
    xs = xbc[..., :A_INNER].reshape(b, L, A_GROUPS, A_HPG, A_HEAD_DIM)
    bm = xbc[..., A_INNER:A_INNER + A_GN].reshape(b, L, A_GROUPS, A_D_STATE)
    cm = xbc[..., A_INNER + A_GN:].reshape(b, L, A_GROUPS, A_D_STATE)
    dtf = jax.nn.softplus(dt_raw[..., :A_HEADS] + p["a_dt_bias_f"].astype(f32)).reshape(b, L, A_GROUPS, A_HPG)
    dtb = jax.nn.softplus(dt_raw[..., A_HEADS:] + p["a_dt_bias_b"].astype(f32)).reshape(b, L, A_GROUPS, A_HPG)
    af = -jnp.exp(p["a_log_f"].astype(f32)).reshape(A_GROUPS, A_HPG)
    ab = -jnp.exp(p["a_log_b"].astype(f32)).reshape(A_GROUPS, A_HPG)
    sshape = (b, A_GROUPS, A_HPG, A_HEAD_DIM, A_D_STATE)
    h0f = h0f.astype(f32).reshape(sshape)
    h0b = h0b.astype(f32).reshape(sshape)
    yf, hf = ssd_chunked_scan(xs, dtf, af, bm, cm, h0f)
    rev = lambda t: jnp.flip(t, axis=1)
    yb, hb = ssd_chunked_scan(rev(xs), rev(dtb), ab, rev(bm), rev(cm), h0b)
    y = yf + rev(yb) + p["a_d"].astype(f32).reshape(A_GROUPS, A_HPG)[:, :, None] * xs
    y = y.reshape(b, L, A_INNER) * jax.nn.silu(z.astype(f32))
    yg = y.reshape(b, L, A_GROUPS, A_INNER // A_GROUPS)
    yg = yg * lax.rsqrt(jnp.mean(jnp.square(yg), axis=-1, keepdims=True) + RMS_EPS)
    y = yg.reshape(b, L, A_INNER) * p["a_norm_w"].astype(f32)
    out_shape = (b, A_HEADS, A_HEAD_DIM, A_D_STATE)
    return y.astype(pa.dtype), hf.reshape(out_shape), hb.reshape(out_shape)


def conformer_conv_mixer(pb, p):
    val, glu_gate, gate = jnp.split(pb, 3, axis=-1)
    u = val * jax.nn.sigmoid(glu_gate)
    u = depthwise_conv(u, p["b_conv_w"], p["b_conv_b"], B_CONV // 2, B_CONV // 2)
    u = jax.nn.silu(layer_norm(u, p["b_ln_g"], p["b_ln_b"]))
    return u * jax.nn.silu(gate)


def chunk_gmlp_mixer(pc, p):
    u, v, gate = jnp.split(pc, 3, axis=-1)
    v = layer_norm(v, p["c_ln_g"], p["c_ln_b"])
    b, L, _ = v.shape
    vc = v.reshape(b, L // C_CHUNK, C_CHUNK, C_GROUPS, C_GROUP_DIM)
    s = jnp.einsum("gij,bcjgd->bcigd", p["c_ws"].astype(v.dtype), vc)
    s = s + p["c_bs"].T.astype(v.dtype)[:, :, None]
    s = s.reshape(b, L, C_WIDTH)
    return u * s * jax.nn.silu(gate)


def even_layer(x, cond, h0f, h0b, p):
    shift, scale, gate = adaln(cond, p["ada_w"], p["ada_b"])
    h = x * (1 + scale) + shift
    proj = h @ p["w_in"]
    ya, hf, hb = ssd_mixer(proj[..., :A_IN], h0f, h0b, p)
    yb = conformer_conv_mixer(proj[..., A_IN:], p)
    y = jnp.concatenate([ya, yb], axis=-1) @ p["w_out"]
    x = layer_norm(ALPHA * x + gate * y, p["ln_g"], p["ln_b"])
    return x, hf, hb


def odd_layer(x, cond, p):
    shift, scale, gate = adaln(cond, p["ada_w"], p["ada_b"])
    h = x * (1 + scale) + shift
    y = chunk_gmlp_mixer(h @ p["w_in"], p) @ p["w_out"]
    return layer_norm(ALPHA * x + gate * y, p["ln_g"], p["ln_b"])


def setup_inputs(seed: int = 0) -> dict:
    key = jax.random.key(seed)
    ks = iter(jax.random.split(key, 48))
    nrm = lambda shape, scale: jax.random.normal(next(ks), shape, jnp.float32) * scale

    def dt_bias():
        dt = jnp.exp(jax.random.uniform(next(ks), (A_HEADS,), jnp.float32, math.log(1e-3), math.log(1e-1)))
        return dt + jnp.log(-jnp.expm1(-dt))

    def a_log():
        return jnp.log(jax.random.uniform(next(ks), (A_HEADS,), jnp.float32, 1.0, 16.0))

    d_in = D_MODEL ** -0.5
    return {
        "x_prompt": nrm((BATCH, SEQ, D_MODEL), 1.0),
        "x_sample": nrm((DEC_BATCH, DEC_SEQ, D_MODEL), 1.0),
        "state_ssd_fwd_l0": nrm((DEC_BATCH, A_HEADS, A_HEAD_DIM, A_D_STATE), 0.5),
        "state_ssd_bwd_l0": nrm((DEC_BATCH, A_HEADS, A_HEAD_DIM, A_D_STATE), 0.5),
        "c": nrm((DEC_BATCH, D_MODEL), 1.0),
        "c_ctx": nrm((D_MODEL,), 1.0),
        "ada_w_l0": nrm((D_MODEL, 3 * D_MODEL), d_in),
        "ada_b_l0": nrm((3 * D_MODEL,), 0.02),
        "w_in_l0": nrm((D_MODEL, L0_IN), d_in),
        "a_conv_w_l0": nrm((A_CONV, A_CONV_CH), A_CONV ** -0.5),
        "a_conv_b_l0": nrm((A_CONV_CH,), 0.02),
        "a_dt_bias_f_l0": dt_bias(),
        "a_dt_bias_b_l0": dt_bias(),
        "a_log_f_l0": a_log(),
        "a_log_b_l0": a_log(),
        "a_d_l0": 1.0 + nrm((A_HEADS,), 0.02),
        "a_norm_w_l0": 1.0 + nrm((A_INNER,), 0.02),
        "b_conv_w_l0": nrm((B_CONV, B_CH), B_CONV ** -0.5),
        "b_conv_b_l0": nrm((B_CH,), 0.02),
        "b_ln_g_l0": 1.0 + nrm((B_CH,), 0.02),
        "b_ln_b_l0": nrm((B_CH,), 0.02),
        "w_out_l0": nrm((L0_OUT, D_MODEL), L0_OUT ** -0.5 * BETA),
        "ln_g_l0": 1.0 + nrm((D_MODEL,), 0.02),
        "ln_b_l0": nrm((D_MODEL,), 0.02),
        "ada_w_l1": nrm((D_MODEL, 3 * D_MODEL), d_in),
        "ada_b_l1": nrm((3 * D_MODEL,), 0.02),
        "w_in_l1": nrm((D_MODEL, L1_IN), d_in),
        "c_ln_g_l1": 1.0 + nrm((C_WIDTH,), 0.02),
        "c_ln_b_l1": nrm((C_WIDTH,), 0.02),
        "c_ws_l1": nrm((C_GROUPS, C_CHUNK, C_CHUNK), C_CHUNK ** -0.5),
        "c_bs_l1": 1.0 + nrm((C_GROUPS, C_CHUNK), 0.02),
        "w_out_l1": nrm((C_WIDTH, D_MODEL), C_WIDTH ** -0.5 * BETA),
        "ln_g_l1": 1.0 + nrm((D_MODEL,), 0.02),
        "ln_b_l1": nrm((D_MODEL,), 0.02),
    }


def reference(x_prompt, x_sample, state_ssd_fwd_l0, state_ssd_bwd_l0, c, c_ctx,
              ada_w_l0, ada_b_l0, w_in_l0, a_conv_w_l0, a_conv_b_l0,
              a_dt_bias_f_l0, a_dt_bias_b_l0, a_log_f_l0, a_log_b_l0, a_d_l0, a_norm_w_l0,
              b_conv_w_l0, b_conv_b_l0, b_ln_g_l0, b_ln_b_l0, w_out_l0, ln_g_l0, ln_b_l0,
              ada_w_l1, ada_b_l1, w_in_l1, c_ln_g_l1, c_ln_b_l1, c_ws_l1, c_bs_l1,
              w_out_l1, ln_g_l1, ln_b_l1):
    layers = [
        dict(ada_w=ada_w_l0, ada_b=ada_b_l0, w_in=w_in_l0, a_conv_w=a_conv_w_l0, a_conv_b=a_conv_b_l0,
             a_dt_bias_f=a_dt_bias_f_l0, a_dt_bias_b=a_dt_bias_b_l0, a_log_f=a_log_f_l0, a_log_b=a_log_b_l0,
             a_d=a_d_l0, a_norm_w=a_norm_w_l0, b_conv_w=b_conv_w_l0, b_conv_b=b_conv_b_l0,
             b_ln_g=b_ln_g_l0, b_ln_b=b_ln_b_l0, w_out=w_out_l0, ln_g=ln_g_l0, ln_b=ln_b_l0),
        dict(ada_w=ada_w_l1, ada_b=ada_b_l1, w_in=w_in_l1, c_ln_g=c_ln_g_l1, c_ln_b=c_ln_b_l1,
             c_ws=c_ws_l1, c_bs=c_bs_l1, w_out=w_out_l1, ln_g=ln_g_l1, ln_b=ln_b_l1),
    ]
    cache_states = [(state_ssd_fwd_l0, state_ssd_bwd_l0)]
    cond_ctx = c_ctx[None, :]
    x_p = x_prompt
    x_s = x_sample + grid_pos_embed(x_sample.shape[1], D_MODEL).astype(x_sample.dtype)
    new_states = []
    for i in range(DEPTH):
        p = layers[i]
        if i % 2 == 0:
            zeros = jnp.zeros((x_p.shape[0], A_HEADS, A_HEAD_DIM, A_D_STATE), jnp.float32)
            x_p, hf, hb = even_layer(x_p, cond_ctx, zeros, zeros, p)
            new_states.append((hf, hb))
            cf, cb = cache_states[i // 2]
            x_s, _, _ = even_layer(x_s, c, cf, cb, p)
        else:
            x_p = odd_layer(x_p, cond_ctx, p)
            x_s = odd_layer(x_s, c, p)
    new_ssd_fwd_l0, new_ssd_bwd_l0 = new_states[0]
    return (x_p, x_s, new_ssd_fwd_l0, new_ssd_bwd_l0)
```

```python
import functools
import math

import jax
import jax.numpy as jnp
from jax import lax
from jax.experimental import pallas as pl
from jax.experimental.pallas import tpu as pltpu

F32 = jnp.float32
BF16 = jnp.bfloat16
HIGHEST = lax.Precision.HIGHEST

D_MODEL = 1024
GRID_W = 64
POS_BASE = 10000.0
A_INNER = 1024
A_HEAD_DIM = 64
A_HEADS = 16
A_GROUPS = 4
A_HPG = 4
A_D_STATE = 128
A_GN = 512
A_CONV = 4
A_GROUP_W = A_HPG * A_HEAD_DIM
B_CH = 1024
B_CONV = 31
B_BLK = 256
C_WIDTH = 2048
C_GROUPS = 16
C_GROUP_DIM = 128
CHUNK = 128
DEPTH = 2
ALPHA = (2 * DEPTH) ** 0.25
LN_EPS = 1e-5
RMS_EPS = 1e-5
HS_LANES = 128
ADA_ROWS = 16
VMEM_LIMIT = 58 * 1024 * 1024


def _dot(a, b):
    return jnp.dot(a, b, preferred_element_type=F32)


def _dot_nt(a, b):
    return lax.dot_general(a, b, (((1,), (1,)), ((), ())), preferred_element_type=F32)


def _layer_norm_rows(t, g, b):
    mu = jnp.mean(t, axis=-1, keepdims=True)
    tc = t - mu
    var = jnp.mean(tc * tc, axis=-1, keepdims=True)
    return tc * lax.rsqrt(var + LN_EPS) * g + b


def _expand(v, e):
    hi = v.astype(BF16)
    lo = (v - hi.astype(F32)).astype(BF16)
    return _dot(hi, e) + _dot(lo, e)


def _rows(c):
    return pl.ds(pl.multiple_of(c * CHUNK, CHUNK), CHUNK)


def _ada_kernel(cond_ref, w0_ref, b0_ref, w1_ref, b1_ref, o0_ref, o1_ref):
    s = jax.nn.silu(cond_ref[...])
    o0_ref[...] = jnp.dot(s, w0_ref[...], preferred_element_type=F32, precision=HIGHEST) + b0_ref[...]
    o1_ref[...] = jnp.dot(s, w1_ref[...], preferred_element_type=F32, precision=HIGHEST) + b1_ref[...]


def _ada_call(cond, w0, b0, w1, b1):
    bn = 768
    n = 3 * D_MODEL
    wspec = pl.BlockSpec((D_MODEL, bn), lambda j: (0, j))
    bspec = pl.BlockSpec((1, bn), lambda j: (0, j))
    ospec = pl.BlockSpec((ADA_ROWS, bn), lambda j: (0, j))
    return pl.pallas_call(
        _ada_kernel,
        grid=(n // bn,),
        in_specs=[pl.BlockSpec((ADA_ROWS, D_MODEL), lambda j: (0, 0)), wspec, bspec, wspec, bspec],
        out_specs=[ospec, ospec],
        out_shape=[jax.ShapeDtypeStruct((ADA_ROWS, n), F32)] * 2,
        compiler_params=pltpu.CompilerParams(dimension_semantics=("arbitrary",)),
        name="adaln",
    )(cond, w0, b0.reshape(1, n), w1, b1.reshape(1, n))


def _x_chunk(x_ref, pos_refs, c):
    xc = x_ref[0, _rows(c), :]
    if pos_refs is not None:
        er_ref, ec_ref = pos_refs
        r0 = jnp.tile(er_ref[2 * c], (GRID_W // 8, 1))
        r1 = jnp.tile(er_ref[2 * c + 1], (GRID_W // 8, 1))
        left = jnp.concatenate([r0, r1], axis=0)
        right = jnp.concatenate([ec_ref[...], ec_ref[...]], axis=0)
        xc = xc + jnp.concatenate([left, right], axis=1)
    return xc


def _modulate_phase(nc, x_ref, pos_refs, mods_ref, h_s):
    shift = mods_ref[0, :, 0:D_MODEL]
    scale1 = 1.0 + mods_ref[0, :, D_MODEL:2 * D_MODEL]

    def body(c, carry):
        h_s[_rows(c), :] = (_x_chunk(x_ref, pos_refs, c) * scale1 + shift).astype(BF16)
        return carry
    lax.fori_loop(0, nc, body, 0)


def _out_phase(nc, x_ref, pos_refs, mods_ref, ycat_s, wout_ref, lng_ref, lnb_ref, out_ref):
    gate = mods_ref[0, :, 2 * D_MODEL:3 * D_MODEL]
    g = lng_ref[...]
    b = lnb_ref[...]

    def body(c, carry):
        y = _dot(ycat_s[_rows(c), :], wout_ref[...])
        t = ALPHA * _x_chunk(x_ref, pos_refs, c) + gate * y
        out_ref[0, _rows(c), :] = _layer_norm_rows(t, g, b)
        return carry
    lax.fori_loop(0, nc, body, 0)


def _l0_kernel(L, has_pos, has_h0, emit_states, *refs):
    nc = L // CHUNK
    it = iter(refs)
    x_ref = next(it)
    mods_ref = next(it)
    pos_refs = (next(it), next(it)) if has_pos else None
    h0_refs = (next(it), next(it)) if has_h0 else None
    (wa_ref, wdt_ref, wb_ref, acw_ref, acb_ref, dtb_ref, alog_ref, dexp_ref, nw_ref,
     bcw_ref, bcb_ref, blg_ref, blb_ref, wout_ref, lng_ref, lnb_ref) = [next(it) for _ in range(16)]
    out_ref = next(it)
    st_refs = (next(it), next(it)) if emit_states else None
    h_s, ycat_s, dt_s, acs_s, dtT_s, acsT_s = [next(it) for _ in range(6)]

    _modulate_phase(nc, x_ref, pos_refs, mods_ref, h_s)

    ii = lax.broadcasted_iota(jnp.int32, (CHUNK, CHUNK), 0)
    jj = lax.broadcasted_iota(jnp.int32, (CHUNK, CHUNK), 1)
    lane_hs = lax.broadcasted_iota(jnp.int32, (1, HS_LANES), 1)
    is_fwd_lane = lane_hs < A_HEADS

    dt_s[...] = _dot(h_s[...], wdt_ref[...])
    tril = (ii >= jj).astype(F32)
    triu = (ii <= jj).astype(F32)
    a_neg = -jnp.exp(alog_ref[...])

    def dt_body(c, carry):
        dt = jax.nn.softplus(dt_s[_rows(c), :] + dtb_ref[...])
        dt_s[_rows(c), :] = dt
        adt = dt * a_neg
        cum_f = jnp.dot(tril, adt, preferred_element_type=F32, precision=HIGHEST)
        cum_b = jnp.dot(triu, adt, preferred_element_type=F32, precision=HIGHEST)
        acs = jnp.where(is_fwd_lane, cum_f, cum_b)
        acs_s[_rows(c), :] = acs
        acsT_s[c] = acs.T
        dtT_s[c] = dt.T
        return carry
    lax.fori_loop(0, nc, dt_body, 0)

    lane_g = lax.broadcasted_iota(jnp.int32, (CHUNK, A_GROUP_W), 1) // A_HEAD_DIM
    exp_row = lax.broadcasted_iota(jnp.int32, (HS_LANES, A_GROUP_W), 0)
    exp_col = lax.broadcasted_iota(jnp.int32, (HS_LANES, A_GROUP_W), 1) // A_HEAD_DIM

    def ssd_group(g, pz_s, xbc_s, xs_s, b_s, c_s, bT_s, hin_s, hT_s):
        e_f = (exp_row == A_HPG * g + exp_col).astype(BF16)
        e_b = (exp_row == A_HEADS + A_HPG * g + exp_col).astype(BF16)
        zc = slice(A_GROUP_W * g, A_GROUP_W * (g + 1))
        xcols = slice(A_INNER + A_GROUP_W * g, A_INNER + A_GROUP_W * (g + 1))
        bcols = slice(2 * A_INNER + A_D_STATE * g, 2 * A_INNER + A_D_STATE * (g + 1))
        ccols = slice(2 * A_INNER + A_GN + A_D_STATE * g, 2 * A_INNER + A_GN + A_D_STATE * (g + 1))
        pz_s[...] = _dot(h_s[...], wa_ref[:, zc])
        xbc_s[pl.ds(0, 8), :] = jnp.zeros((8, 2 * A_GROUP_W), F32)
        xbc_s[pl.ds(8 + L, 8), :] = jnp.zeros((8, 2 * A_GROUP_W), F32)
        xbc_s[pl.ds(8, L), 0:A_GROUP_W] = _dot(h_s[...], wa_ref[:, xcols])
        xbc_s[pl.ds(8, L), A_GROUP_W:A_GROUP_W + A_D_STATE] = _dot(h_s[...], wa_ref[:, bcols])
        xbc_s[pl.ds(8, L), A_GROUP_W + A_D_STATE:2 * A_GROUP_W] = _dot(h_s[...], wa_ref[:, ccols])

        def conv_cols(ref):
            x0 = A_GROUP_W * g
            b0 = A_INNER + A_D_STATE * g
            c0 = A_INNER + A_GN + A_D_STATE * g
            return jnp.concatenate([ref[:, x0:x0 + A_GROUP_W], ref[:, b0:b0 + A_D_STATE],
                                    ref[:, c0:c0 + A_D_STATE]], axis=1)
        cw = conv_cols(acw_ref)
        cbias = conv_cols(acb_ref)

        def conv_body(c, carry):
            win = xbc_s[pl.ds(pl.multiple_of(c * CHUNK, CHUNK), CHUNK + 16), :]
            acc = cbias + win[6:6 + CHUNK] * cw[0:1]
            for k in range(1, A_CONV):
                acc = acc + win[6 + k:6 + k + CHUNK] * cw[k:k + 1]
            act = jax.nn.silu(acc)
            xs_s[_rows(c), :] = act[:, 0:A_GROUP_W]
            bm = act[:, A_GROUP_W:A_GROUP_W + A_D_STATE]
            b_s[_rows(c), :] = bm.astype(BF16)
            bT_s[c] = bm.T.astype(BF16)
            c_s[_rows(c), :] = act[:, A_GROUP_W + A_D_STATE:].astype(BF16)
            return carry
        lax.fori_loop(0, nc, conv_body, 0)

        def chunk_scalars(c):
            acs = acs_s[_rows(c), :]
            dt = dt_s[_rows(c), :]
            ref_row = jnp.where(is_fwd_lane, acs[CHUNK - 1:CHUNK, :], acs[0:1, :])
            decay_end = jnp.exp(jnp.minimum(ref_row - acs, 0.0))
            chunk_decay = jnp.broadcast_to(jnp.exp(jnp.minimum(ref_row, 0.0)), (8, HS_LANES))
            return acs, dt, decay_end * dt, chunk_decay

        if has_h0:
            hT_s[...] = h0_refs[1][0, A_HPG * g:A_HPG * (g + 1)].reshape(A_GROUP_W, A_D_STATE).T
        else:
            hT_s[...] = jnp.zeros((A_D_STATE, A_GROUP_W), F32)

        def bwd_body(t, carry):
            c = nc - 1 - t
            hin_s[c] = hT_s[...]
            _, _, w, chunk_decay = chunk_scalars(c)
            xw = (xs_s[_rows(c), :] * _expand(w, e_b)).astype(BF16)
            cd = _expand(chunk_decay, e_b)[0:1, :]
            hT_s[...] = hT_s[...] * cd + _dot(bT_s[c], xw)
            return carry
        lax.fori_loop(0, nc, bwd_body, 0)
        if emit_states:
            st_refs[1][0, A_HPG * g:A_HPG * (g + 1)] = hT_s[...].T.reshape(A_HPG, A_HEAD_DIM, A_D_STATE)

        if has_h0:
            hT_s[...] = h0_refs[0][0, A_HPG * g:A_HPG * (g + 1)].reshape(A_GROUP_W, A_D_STATE).T
        else:
            hT_s[...] = jnp.zeros((A_D_STATE, A_GROUP_W), F32)
        d_skip = dexp_ref[:, zc]
        nw = nw_ref[:, zc]

        def fwd_body(c, carry):
            acs, dt, w, chunk_decay = chunk_scalars(c)
            acsT = acsT_s[c]
            dtT = dtT_s[c]
            xs = xs_s[_rows(c), :]
            xs_bf = xs.astype(BF16)
            bmat = b_s[_rows(c), :]
            cmat = c_s[_rows(c), :]
            cbm = _dot_nt(cmat, bmat)
            y = d_skip * xs
            for r in range(A_HPG):
                hf = A_HPG * g + r
                hb = A_HEADS + hf
                seg = jnp.where(ii >= jj, acs[:, hf:hf + 1] - acsT[hf:hf + 1, :],
                                acs[:, hb:hb + 1] - acsT[hb:hb + 1, :])
                dtf = dtT[hf:hf + 1, :]
                dtb = dtT[hb:hb + 1, :]
                wd = jnp.where(ii > jj, dtf, jnp.where(ii < jj, dtb, dtf + dtb))
                m = (cbm * jnp.exp(seg) * wd).astype(BF16)
                y = y + _dot(m, jnp.where(lane_g == r, xs_bf, jnp.zeros_like(xs_bf)))
            e_in = jnp.exp(jnp.minimum(acs, 0.0))
            hT = hT_s[...]
            y = y + _expand(e_in, e_f) * _dot(cmat, hT.astype(BF16))
            y = y + _expand(e_in, e_b) * _dot(cmat, hin_s[c].astype(BF16))
            xw = (xs * _expand(w, e_f)).astype(BF16)
            cd = _expand(chunk_decay, e_f)[0:1, :]
            hT_s[...] = hT * cd + _dot(bT_s[c], xw)
            y = y * jax.nn.silu(pz_s[_rows(c), :])
            y = y * lax.rsqrt(jnp.mean(y * y, axis=-1, keepdims=True) + RMS_EPS) * nw
            ycat_s[_rows(c), zc] = y.astype(BF16)
            return carry
        lax.fori_loop(0, nc, fwd_body, 0)
        if emit_states:
            st_refs[0][0, A_HPG * g:A_HPG * (g + 1)] = hT_s[...].T.reshape(A_HPG, A_HEAD_DIM, A_D_STATE)

    for g in range(A_GROUPS):
        pl.run_scoped(
            functools.partial(ssd_group, g),
            pltpu.VMEM((L, A_GROUP_W), F32),
            pltpu.VMEM((L + 16, 2 * A_GROUP_W), F32),
            pltpu.VMEM((L, A_GROUP_W), F32),
            pltpu.VMEM((L, A_D_STATE), BF16),
            pltpu.VMEM((L, A_D_STATE), BF16),
            pltpu.VMEM((nc, A_D_STATE, CHUNK), BF16),
            pltpu.VMEM((nc, A_D_STATE, A_GROUP_W), F32),
            pltpu.VMEM((A_D_STATE, A_GROUP_W), F32),
        )

    def conformer(pa_s, pb_s, ubuf_s, uc_s, sg_s):
        ubuf_s[pl.ds(0, 16), :] = jnp.zeros((16, B_BLK), F32)
        ubuf_s[pl.ds(16 + L, 16), :] = jnp.zeros((16, B_BLK), F32)
        for kb in range(B_CH // B_BLK):
            cols = slice(B_BLK * kb, B_BLK * (kb + 1))
            pa_s[...] = _dot(h_s[...], wb_ref[:, B_BLK * kb:B_BLK * (kb + 1)])
            pb_s[...] = _dot(h_s[...], wb_ref[:, B_CH + B_BLK * kb:B_CH + B_BLK * (kb + 1)])

            def glu_body(c, carry):
                ubuf_s[pl.ds(pl.multiple_of(16 + c * CHUNK, 8), CHUNK), :] = (
                    pa_s[_rows(c), :] * jax.nn.sigmoid(pb_s[_rows(c), :]))
                return carry
            lax.fori_loop(0, nc, glu_body, 0)
            pb_s[...] = _dot(h_s[...], wb_ref[:, 2 * B_CH + B_BLK * kb:2 * B_CH + B_BLK * (kb + 1)])

            def conv_body(c, carry):
                sg_s[_rows(c), cols] = jax.nn.silu(pb_s[_rows(c), :])
                for lt in range(B_BLK // 128):
                    lanes = slice(128 * lt, 128 * (lt + 1))
                    wl = slice(B_BLK * kb + 128 * lt, B_BLK * kb + 128 * (lt + 1))
                    win = ubuf_s[pl.ds(pl.multiple_of(c * CHUNK, CHUNK), CHUNK + 32), lanes]
                    acc = jnp.broadcast_to(bcb_ref[:, wl], (CHUNK, 128))
                    for r in range(8):
                        taps = [k for k in range(B_CONV) if (k + 1) % 8 == r]
                        sh = win[r:r + CHUNK + 24]
                        for k in taps:
                            q = (k + 1) // 8
                            acc = acc + sh[8 * q:8 * q + CHUNK] * bcw_ref[k:k + 1, wl]
                    uc_s[_rows(c), wl] = acc
                return carry
            lax.fori_loop(0, nc, conv_body, 0)

        g = blg_ref[...]
        b = blb_ref[...]

        def ln_body(c, carry):
            u = jax.nn.silu(_layer_norm_rows(uc_s[_rows(c), :], g, b))
            ycat_s[_rows(c), A_INNER:A_INNER + B_CH] = (u * sg_s[_rows(c), :]).astype(BF16)
            return carry
        lax.fori_loop(0, nc, ln_body, 0)

    pl.run_scoped(
        conformer,
        pltpu.VMEM((L, B_BLK), F32),
        pltpu.VMEM((L, B_BLK), F32),
        pltpu.VMEM((L + 32, B_BLK), F32),
        pltpu.VMEM((L, B_CH), F32),
        pltpu.VMEM((L, B_CH), F32),
    )

    _out_phase(nc, x_ref, pos_refs, mods_ref, ycat_s, wout_ref, lng_ref, lnb_ref, out_ref)


def _const_spec(shape):
    nd = len(shape)
    return pl.BlockSpec(shape, lambda s: (0,) * nd, pipeline_mode=pl.Buffered(1))


def _l0_call(x, mods, mod_row0, pos_tabs, h0, weights, emit_states):
    nb, L, _ = x.shape
    nc = L // CHUNK
    has_pos = pos_tabs is not None
    has_h0 = h0 is not None
    row_of = (lambda s: (s + mod_row0, 0, 0)) if mod_row0 else (lambda s: (0, 0, 0))
    in_specs = [pl.BlockSpec((1, L, D_MODEL), lambda s: (s, 0, 0)),
                pl.BlockSpec((1, 1, 3 * D_MODEL), row_of)]
    args = [x, mods]
    if has_pos:
        in_specs += [_const_spec(t.shape) for t in pos_tabs]
        args += list(pos_tabs)
    st_spec = pl.BlockSpec((1, A_HEADS, A_HEAD_DIM, A_D_STATE), lambda s: (s, 0, 0, 0))
    if has_h0:
        in_specs += [st_spec, st_spec]
        args += list(h0)
    in_specs += [_const_spec(w.shape) for w in weights]
    args += list(weights)
    out_specs = [pl.BlockSpec((1, L, D_MODEL), lambda s: (s, 0, 0))]
    out_shape = [jax.ShapeDtypeStruct((nb, L, D_MODEL), F32)]
    if emit_states:
        out_specs += [st_spec, st_spec]
        out_shape += [jax.ShapeDtypeStruct((nb, A_HEADS, A_HEAD_DIM, A_D_STATE), F32)] * 2
    scratch = [pltpu.VMEM((L, D_MODEL), BF16),
               pltpu.VMEM((L, 2 * D_MODEL), BF16),
               pltpu.VMEM((L, HS_LANES), F32),
               pltpu.VMEM((L, HS_LANES), F32),
               pltpu.VMEM((nc, HS_LANES, CHUNK), F32),
               pltpu.VMEM((nc, HS_LANES, CHUNK), F32)]
    return pl.pallas_call(
        functools.partial(_l0_kernel, L, has_pos, has_h0, emit_states),
        grid=(nb,),
        in_specs=in_specs,
        out_specs=out_specs,
        out_shape=out_shape,
        scratch_shapes=scratch,
        compiler_params=pltpu.CompilerParams(dimension_semantics=("arbitrary",), vmem_limit_bytes=VMEM_LIMIT),
        name="layer0_L%d" % L,
    )(*args)


def _l1_kernel(L, has_pos, *refs):
    nc = L // CHUNK
    (x_ref, mods_ref, wu_ref, wv_ref, wg_ref, cg_ref, cb_ref, ws_ref, bs_ref,
     wout_ref, lng_ref, lnb_ref, out_ref, h_s, ycat_s, vn_s) = refs
    _modulate_phase(nc, x_ref, None, mods_ref, h_s)

    def v_phase(pv_s):
        g = cg_ref[...]
        b = cb_ref[...]
        rb = min(L, 512)
        for r0 in range(0, L, rb):
            pv_s[...] = _dot(h_s[pl.ds(r0, rb), :], wv_ref[...])

            def ln_body(c, carry):
                vn_s[pl.ds(pl.multiple_of(r0 + c * CHUNK, CHUNK), CHUNK), :] = (
                    _layer_norm_rows(pv_s[_rows(c), :], g, b).astype(BF16))
                return carry
            lax.fori_loop(0, rb // CHUNK, ln_body, 0)
    pl.run_scoped(v_phase, pltpu.VMEM((min(L, 512), C_WIDTH), F32))

    def gate_phase(pu_s, pg_s):
        cblk = 2 * C_GROUP_DIM
        for kb in range(C_WIDTH // cblk):
            cols = slice(cblk * kb, cblk * (kb + 1))
            pu_s[...] = _dot(h_s[...], wu_ref[:, cols])
            pg_s[...] = _dot(h_s[...], wg_ref[:, cols])

            def body(c, carry):
                parts = []
                for q in range(2):
                    grp = 2 * kb + q
                    vg = vn_s[_rows(c), C_GROUP_DIM * grp:C_GROUP_DIM * (grp + 1)]
                    parts.append(_dot(ws_ref[grp], vg))
                s = jnp.concatenate(parts, axis=1) + bs_ref[:, cols]
                ycat_s[_rows(c), cols] = (pu_s[_rows(c), :] * s * jax.nn.silu(pg_s[_rows(c), :])).astype(BF16)
                return carry
            lax.fori_loop(0, nc, body, 0)
    pl.run_scoped(gate_phase, pltpu.VMEM((L, 2 * C_GROUP_DIM), F32), pltpu.VMEM((L, 2 * C_GROUP_DIM), F32))

    _out_phase(nc, x_ref, None, mods_ref, ycat_s, wout_ref, lng_ref, lnb_ref, out_ref)


def _l1_call(x, mods, mod_row0, weights):
    nb, L, _ = x.shape
    row_of = (lambda s: (s + mod_row0, 0, 0)) if mod_row0 else (lambda s: (0, 0, 0))
    in_specs = [pl.BlockSpec((1, L, D_MODEL), lambda s: (s, 0, 0)),
                pl.BlockSpec((1, 1, 3 * D_MODEL), row_of)]
    in_specs += [_const_spec(w.shape) for w in weights]
    scratch = [pltpu.VMEM((L, D_MODEL), BF16),
               pltpu.VMEM((L, C_WIDTH), BF16),
               pltpu.VMEM((L, C_WIDTH), BF16)]
    return pl.pallas_call(
        functools.partial(_l1_kernel, L, False),
        grid=(nb,),
        in_specs=in_specs,
        out_specs=pl.BlockSpec((1, L, D_MODEL), lambda s: (s, 0, 0)),
        out_shape=jax.ShapeDtypeStruct((nb, L, D_MODEL), F32),
        scratch_shapes=scratch,
        compiler_params=pltpu.CompilerParams(dimension_semantics=("arbitrary",), vmem_limit_bytes=VMEM_LIMIT),
        name="layer1_L%d" % L,
    )(x, mods, *weights)


def _pos_tables(L):
    rows = L // GRID_W
    quarter = D_MODEL // 4
    freqs = 1.0 / (POS_BASE ** (jnp.arange(quarter, dtype=F32) / quarter))
    r = jnp.arange(rows, dtype=F32)[:, None] * freqs
    cc = jnp.arange(GRID_W, dtype=F32)[:, None] * freqs
    emb_r = jnp.concatenate([jnp.sin(r), jnp.cos(r)], axis=-1)
    emb_c = jnp.concatenate([jnp.sin(cc), jnp.cos(cc)], axis=-1)
    return jnp.broadcast_to(emb_r[:, None, :], (rows, 8, D_MODEL // 2)), emb_c


def _hs_row(f, b):
    return jnp.pad(jnp.concatenate([f, b]).astype(F32), (0, HS_LANES - 2 * A_HEADS)).reshape(1, HS_LANES)


def kernel(x_prompt, x_sample, state_ssd_fwd_l0, state_ssd_bwd_l0, c, c_ctx, ada_w_l0, ada_b_l0, w_in_l0, a_conv_w_l0, a_conv_b_l0, a_dt_bias_f_l0, a_dt_bias_b_l0, a_log_f_l0, a_log_b_l0, a_d_l0, a_norm_w_l0, b_conv_w_l0, b_conv_b_l0, b_ln_g_l0, b_ln_b_l0, w_out_l0, ln_g_l0, ln_b_l0, ada_w_l1, ada_b_l1, w_in_l1, c_ln_g_l1, c_ln_b_l1, c_ws_l1, c_bs_l1, w_out_l1, ln_g_l1, ln_b_l1):
    dec_batch = c.shape[0]
    cond = jnp.concatenate([c_ctx[None, :], c, jnp.zeros((ADA_ROWS - 1 - dec_batch, D_MODEL), F32)], axis=0)
    mods0, mods1 = _ada_call(cond, ada_w_l0, ada_b_l0, ada_w_l1, ada_b_l1)
    mods0 = mods0.reshape(ADA_ROWS, 1, 3 * D_MODEL)
    mods1 = mods1.reshape(ADA_ROWS, 1, 3 * D_MODEL)

    a_in = A_INNER + A_INNER + 2 * A_GN
    row = lambda v: v.astype(F32).reshape(1, -1)
    w0 = [
        w_in_l0[:, :a_in].astype(BF16),
        jnp.pad(w_in_l0[:, a_in:a_in + 2 * A_HEADS], ((0, 0), (0, HS_LANES - 2 * A_HEADS))).astype(BF16),
        w_in_l0[:, a_in + 2 * A_HEADS:].astype(BF16),
        a_conv_w_l0.astype(F32), row(a_conv_b_l0),
        _hs_row(a_dt_bias_f_l0, a_dt_bias_b_l0), _hs_row(a_log_f_l0, a_log_b_l0),
        row(jnp.repeat(a_d_l0, A_HEAD_DIM)), row(a_norm_w_l0),
        jnp.pad(b_conv_w_l0.astype(F32), ((0, 1), (0, 0))), row(b_conv_b_l0), row(b_ln_g_l0), row(b_ln_b_l0),
        w_out_l0.astype(BF16), row(ln_g_l0), row(ln_b_l0),
    ]
    pos_tabs = _pos_tables(x_sample.shape[1])
    xp1, hf, hb = _l0_call(x_prompt, mods0, 0, None, None, w0, True)
    (xs1,) = _l0_call(x_sample, mods0, 1, pos_tabs, (state_ssd_fwd_l0, state_ssd_bwd_l0), w0, False)

    w1 = [
        w_in_l1[:, :C_WIDTH].astype(BF16), w_in_l1[:, C_WIDTH:2 * C_WIDTH].astype(BF16),
        w_in_l1[:, 2 * C_WIDTH:].astype(BF16),
        row(c_ln_g_l1), row(c_ln_b_l1), c_ws_l1.astype(BF16),
        jnp.repeat(c_bs_l1.astype(F32).T, C_GROUP_DIM, axis=1),
        w_out_l1.astype(BF16), row(ln_g_l1), row(ln_b_l1),
    ]
    y_prompt = _l1_call(xp1, mods1, 0, w1)
    y_sample = _l1_call(xs1, mods1, 1, w1)
    return (y_prompt, y_sample, hf, hb)
```

```python
import functools
import math

import jax
import jax.numpy as jnp
from jax import lax
from jax.experimental import pallas as pl
from jax.experimental.pallas import tpu as pltpu

F32 = jnp.float32
BF16 = jnp.bfloat16
HIGHEST = lax.Precision.HIGHEST

D_MODEL = 1024
GRID_W = 64
POS_BASE = 10000.0
A_INNER = 1024
A_HEAD_DIM = 64
A_HEADS = 16
A_GROUPS = 4
A_HPG = 4
A_D_STATE = 128
A_GN = 512
A_CONV = 4
A_GROUP_W = A_HPG * A_HEAD_DIM
B_CH = 1024
B_CONV = 31
B_BLK = 256
C_WIDTH = 2048
C_GROUPS = 16
C_GROUP_DIM = 128
CHUNK = 128
ROW_TILE = 32
OUT_BLOCK = 512
DEPTH = 2
ALPHA = (2 * DEPTH) ** 0.25
LN_EPS = 1e-5
RMS_EPS = 1e-5
HS_LANES = 128
ADA_ROWS = 16
VMEM_LIMIT = 58 * 1024 * 1024


def _dot(a, b):
    return jnp.dot(a, b, preferred_element_type=F32)


def _dot_nt(a, b):
    return lax.dot_general(a, b, (((1,), (1,)), ((), ())), preferred_element_type=F32)


def _layer_norm_rows(t, g, b):
    mu = jnp.mean(t, axis=-1, keepdims=True)
    tc = t - mu
    var = jnp.mean(tc * tc, axis=-1, keepdims=True)
    return tc * lax.rsqrt(var + LN_EPS) * g + b


def _expand(v, e):
    hi = v.astype(BF16)
    lo = (v - hi.astype(F32)).astype(BF16)
    return _dot(hi, e) + _dot(lo, e)


def _rows(c):
    return pl.ds(pl.multiple_of(c * CHUNK, CHUNK), CHUNK)


def _tile(t):
    return pl.ds(pl.multiple_of(t * ROW_TILE, ROW_TILE), ROW_TILE)


def _ada_kernel(cond_ref, w0_ref, b0_ref, w1_ref, b1_ref, o0_ref, o1_ref):
    s = jax.nn.silu(cond_ref[...])
    o0_ref[...] = jnp.dot(s, w0_ref[...], preferred_element_type=F32, precision=HIGHEST) + b0_ref[...]
    o1_ref[...] = jnp.dot(s, w1_ref[...], preferred_element_type=F32, precision=HIGHEST) + b1_ref[...]


def _ada_call(cond, w0, b0, w1, b1):
    bn = 768
    n = 3 * D_MODEL
    wspec = pl.BlockSpec((D_MODEL, bn), lambda j: (0, j))
    bspec = pl.BlockSpec((1, bn), lambda j: (0, j))
    ospec = pl.BlockSpec((ADA_ROWS, bn), lambda j: (0, j))
    return pl.pallas_call(
        _ada_kernel,
        grid=(n // bn,),
        in_specs=[pl.BlockSpec((ADA_ROWS, D_MODEL), lambda j: (0, 0)), wspec, bspec, wspec, bspec],
        out_specs=[ospec, ospec],
        out_shape=[jax.ShapeDtypeStruct((ADA_ROWS, n), F32)] * 2,
        compiler_params=pltpu.CompilerParams(dimension_semantics=("arbitrary",)),
        name="adaln",
    )(cond, w0, b0.reshape(1, n), w1, b1.reshape(1, n))


def _x_tile(x_ref, pos_refs, t):
    xt = x_ref[0, _tile(t), :]
    if pos_refs is not None:
        er_ref, ec_ref = pos_refs
        tiles_per_grid_row = GRID_W // ROW_TILE
        grid_row = lax.div(t, tiles_per_grid_row)
        col0 = pl.multiple_of(lax.rem(t, tiles_per_grid_row) * ROW_TILE, ROW_TILE)
        left = jnp.tile(er_ref[grid_row], (ROW_TILE // 8, 1))
        xt = xt + jnp.concatenate([left, ec_ref[pl.ds(col0, ROW_TILE), :]], axis=1)
    return xt


def _modulate_phase(L, x_ref, pos_refs, mods_ref, h_s):
    shift = mods_ref[0, :, 0:D_MODEL]
    scale1 = 1.0 + mods_ref[0, :, D_MODEL:2 * D_MODEL]

    def body(t, carry):
        h_s[_tile(t), :] = (_x_tile(x_ref, pos_refs, t) * scale1 + shift).astype(BF16)
        return carry
    lax.fori_loop(0, L // ROW_TILE, body, 0, unroll=2)


def _out_phase(L, x_ref, pos_refs, mods_ref, ycat_s, wout_ref, lng_ref, lnb_ref, out_ref):
    gate = mods_ref[0, :, 2 * D_MODEL:3 * D_MODEL]
    g = lng_ref[...]
    b = lnb_ref[...]
    rb = min(L, OUT_BLOCK)

    def run(y_s):
        for r0 in range(0, L, rb):
            y_s[...] = _dot(ycat_s[pl.ds(r0, rb), :], wout_ref[...])

            def body(t, carry):
                tt = t + r0 // ROW_TILE
                v = ALPHA * _x_tile(x_ref, pos_refs, tt) + gate * y_s[_tile(t), :]
                out_ref[0, _tile(tt), :] = _layer_norm_rows(v, g, b)
                return carry
            lax.fori_loop(0, rb // ROW_TILE, body, 0, unroll=2)
    pl.run_scoped(run, pltpu.VMEM((rb, D_MODEL), F32))


def _l0_kernel(L, has_pos, has_h0, emit_states, *refs):
    nc = L // CHUNK
    it = iter(refs)
    x_ref = next(it)
    mods_ref = next(it)
    pos_refs = (next(it), next(it)) if has_pos else None
    h0_refs = (next(it), next(it)) if has_h0 else None
    (wa_ref, wdt_ref, wb_ref, acw_ref, acb_ref, dtb_ref, alog_ref, dexp_ref, nw_ref,
     bcw_ref, bcb_ref, blg_ref, blb_ref, wout_ref, lng_ref, lnb_ref) = [next(it) for _ in range(16)]
    out_ref = next(it)
    st_refs = (next(it), next(it)) if emit_states else None
    h_s, ycat_s, dt_s, acs_s, dtT_s, acsT_s = [next(it) for _ in range(6)]

    _modulate_phase(L, x_ref, pos_refs, mods_ref, h_s)

    ii = lax.broadcasted_iota(jnp.int32, (CHUNK, CHUNK), 0)
    jj = lax.broadcasted_iota(jnp.int32, (CHUNK, CHUNK), 1)
    lane_hs = lax.broadcasted_iota(jnp.int32, (1, HS_LANES), 1)
    is_fwd_lane = lane_hs < A_HEADS

    dt_s[...] = _dot(h_s[...], wdt_ref[...])
    tril = (ii >= jj).astype(F32)
    triu = (ii <= jj).astype(F32)
    a_neg = -jnp.exp(alog_ref[...])

    def dt_body(c, carry):
        dt = jax.nn.softplus(dt_s[_rows(c), :] + dtb_ref[...])
        dt_s[_rows(c), :] = dt
        adt = dt * a_neg
        cum_f = jnp.dot(tril, adt, preferred_element_type=F32, precision=HIGHEST)
        cum_b = jnp.dot(triu, adt, preferred_element_type=F32, precision=HIGHEST)
        acs = jnp.where(is_fwd_lane, cum_f, cum_b)
        acs_s[_rows(c), :] = acs
        acsT_s[c] = acs.T
        dtT_s[c] = dt.T
        return carry
    lax.fori_loop(0, nc, dt_body, 0, unroll=2)

    lane_g = lax.broadcasted_iota(jnp.int32, (CHUNK, A_GROUP_W), 1) // A_HEAD_DIM
    exp_row = lax.broadcasted_iota(jnp.int32, (HS_LANES, 2 * A_GROUP_W), 0)
    exp_col = lax.broadcasted_iota(jnp.int32, (HS_LANES, 2 * A_GROUP_W), 1)
    exp_src = exp_col // A_HEAD_DIM + jnp.where(exp_col < A_GROUP_W, 0, A_HEADS - A_HPG)

    def ssd_group(g, pz_s, xbc_s, xs_s, b_s, c_s, st_s, cd_s):
        e_fb = (exp_row == A_HPG * g + exp_src).astype(BF16)
        zc = slice(A_GROUP_W * g, A_GROUP_W * (g + 1))
        xcols = slice(A_INNER + A_GROUP_W * g, A_INNER + A_GROUP_W * (g + 1))
        bcols = slice(2 * A_INNER + A_D_STATE * g, 2 * A_INNER + A_D_STATE * (g + 1))
        ccols = slice(2 * A_INNER + A_GN + A_D_STATE * g, 2 * A_INNER + A_GN + A_D_STATE * (g + 1))
        pz_s[...] = _dot(h_s[...], wa_ref[:, zc])
        xbc_s[pl.ds(0, 8), :] = jnp.zeros((8, 2 * A_GROUP_W), F32)
        xbc_s[pl.ds(8 + L, 8), :] = jnp.zeros((8, 2 * A_GROUP_W), F32)
        xbc_s[pl.ds(8, L), 0:A_GROUP_W] = _dot(h_s[...], wa_ref[:, xcols])
        xbc_s[pl.ds(8, L), A_GROUP_W:A_GROUP_W + A_D_STATE] = _dot(h_s[...], wa_ref[:, bcols])
        xbc_s[pl.ds(8, L), A_GROUP_W + A_D_STATE:2 * A_GROUP_W] = _dot(h_s[...], wa_ref[:, ccols])

        def conv_cols(ref):
            x0 = A_GROUP_W * g
            b0 = A_INNER + A_D_STATE * g
            c0 = A_INNER + A_GN + A_D_STATE * g
            return jnp.concatenate([ref[:, x0:x0 + A_GROUP_W], ref[:, b0:b0 + A_D_STATE],
                                    ref[:, c0:c0 + A_D_STATE]], axis=1)
        cw = conv_cols(acw_ref)
        cbias = conv_cols(acb_ref)

        def chunk_scalars(c):
            acs = acs_s[_rows(c), :]
            dt = dt_s[_rows(c), :]
            ref_row = jnp.where(is_fwd_lane, acs[CHUNK - 1:CHUNK, :], acs[0:1, :])
            decay_end = jnp.exp(jnp.minimum(ref_row - acs, 0.0))
            chunk_decay = jnp.broadcast_to(jnp.exp(jnp.minimum(ref_row, 0.0)), (16, HS_LANES))
            return decay_end * dt, chunk_decay

        def conv_body(c, carry):
            win = xbc_s[pl.ds(pl.multiple_of(c * CHUNK, CHUNK), CHUNK + 16), :]
            acc = cbias + win[6:6 + CHUNK] * cw[0:1]
            for k in range(1, A_CONV):
                acc = acc + win[6 + k:6 + k + CHUNK] * cw[k:k + 1]
            act = jax.nn.silu(acc)
            xs = act[:, 0:A_GROUP_W]
            bm = act[:, A_GROUP_W:A_GROUP_W + A_D_STATE]
            xs_s[_rows(c), :] = xs
            b_s[_rows(c), :] = bm.astype(BF16)
            c_s[_rows(c), :] = act[:, A_GROUP_W + A_D_STATE:].astype(BF16)
            w, chunk_decay = chunk_scalars(c)
            wexp = _expand(w, e_fb)
            xw = jnp.concatenate([xs * wexp[:, 0:A_GROUP_W], xs * wexp[:, A_GROUP_W:]], axis=1)
            st_s[c] = _dot(bm.T.astype(BF16), xw.astype(BF16))
            cd_s[c] = _expand(chunk_decay, e_fb)
            return carry
        lax.fori_loop(0, nc, conv_body, 0, unroll=2)

        if has_h0:
            hf0 = h0_refs[0][0, A_HPG * g:A_HPG * (g + 1)].reshape(A_GROUP_W, A_D_STATE).T
            hb0 = h0_refs[1][0, A_HPG * g:A_HPG * (g + 1)].reshape(A_GROUP_W, A_D_STATE).T
        else:
            hf0 = jnp.zeros((A_D_STATE, A_GROUP_W), F32)
            hb0 = hf0

        def scan_body(t, carry):
            hf, hb = carry
            tb = nc - 1 - t
            sf = st_s[t, :, 0:A_GROUP_W]
            st_s[t, :, 0:A_GROUP_W] = hf
            hf = hf * cd_s[t, 0:1, 0:A_GROUP_W] + sf
            sb = st_s[tb, :, A_GROUP_W:]
            st_s[tb, :, A_GROUP_W:] = hb
            hb = hb * cd_s[tb, 0:1, A_GROUP_W:] + sb
            return hf, hb
        hf_fin, hb_fin = lax.fori_loop(0, nc, scan_body, (hf0, hb0))
        if emit_states:
            st_refs[0][0, A_HPG * g:A_HPG * (g + 1)] = hf_fin.T.reshape(A_HPG, A_HEAD_DIM, A_D_STATE)
            st_refs[1][0, A_HPG * g:A_HPG * (g + 1)] = hb_fin.T.reshape(A_HPG, A_HEAD_DIM, A_D_STATE)

        d_skip = dexp_ref[:, zc]
        nw = nw_ref[:, zc]

        def out_body(c, carry):
            acs = acs_s[_rows(c), :]
            acsT = acsT_s[c]
            dtT = dtT_s[c]
            xs = xs_s[_rows(c), :]
            cmat = c_s[_rows(c), :]
            cbm = _dot_nt(cmat, b_s[_rows(c), :])
            y = d_skip * xs
            for r in range(A_HPG):
                hf = A_HPG * g + r
                hb = A_HEADS + hf
                seg = jnp.where(ii >= jj, acs[:, hf:hf + 1] - acsT[hf:hf + 1, :],
                                acs[:, hb:hb + 1] - acsT[hb:hb + 1, :])
                dtf = dtT[hf:hf + 1, :]
                dtb = dtT[hb:hb + 1, :]
                wd = jnp.where(ii > jj, dtf, jnp.where(ii < jj, dtb, dtf + dtb))
                m = (cbm * jnp.exp(seg) * wd).astype(BF16)
                y = y + _dot(m, jnp.where(lane_g == r, xs, 0.0).astype(BF16))
            e_in = _expand(jnp.exp(jnp.minimum(acs, 0.0)), e_fb)
            off = e_in * _dot(cmat, st_s[c].astype(BF16))
            y = y + off[:, 0:A_GROUP_W] + off[:, A_GROUP_W:]
            y = y * jax.nn.silu(pz_s[_rows(c), :])
            y = y * lax.rsqrt(jnp.mean(y * y, axis=-1, keepdims=True) + RMS_EPS) * nw
            ycat_s[_rows(c), zc] = y.astype(BF16)
            return carry
        lax.fori_loop(0, nc, out_body, 0, unroll=2)

    for g in range(A_GROUPS):
        pl.run_scoped(
            functools.partial(ssd_group, g),
            pltpu.VMEM((L, A_GROUP_W), F32),
            pltpu.VMEM((L + 16, 2 * A_GROUP_W), F32),
            pltpu.VMEM((L, A_GROUP_W), F32),
            pltpu.VMEM((L, A_D_STATE), BF16),
            pltpu.VMEM((L, A_D_STATE), BF16),
            pltpu.VMEM((nc, A_D_STATE, 2 * A_GROUP_W), F32),
            pltpu.VMEM((nc, 16, 2 * A_GROUP_W), F32),
        )

    def conformer(pa_s, pb_s, ubuf_s, uc_s, sg_s):
        ubuf_s[pl.ds(0, 16), :] = jnp.zeros((16, B_BLK), F32)
        ubuf_s[pl.ds(16 + L, 16), :] = jnp.zeros((16, B_BLK), F32)
        for kb in range(B_CH // B_BLK):
            cols = slice(B_BLK * kb, B_BLK * (kb + 1))
            pa_s[...] = _dot(h_s[...], wb_ref[:, B_BLK * kb:B_BLK * (kb + 1)])
            pb_s[...] = _dot(h_s[...], wb_ref[:, B_CH + B_BLK * kb:B_CH + B_BLK * (kb + 1)])

            def glu_body(c, carry):
                ubuf_s[pl.ds(pl.multiple_of(16 + c * CHUNK, 8), CHUNK), :] = (
                    pa_s[_rows(c), :] * jax.nn.sigmoid(pb_s[_rows(c), :]))
                return carry
            lax.fori_loop(0, nc, glu_body, 0, unroll=2)
            pb_s[...] = _dot(h_s[...], wb_ref[:, 2 * B_CH + B_BLK * kb:2 * B_CH + B_BLK * (kb + 1)])

            def conv_body(c, carry):
                sg_s[_rows(c), cols] = jax.nn.silu(pb_s[_rows(c), :])
                for lt in range(B_BLK // 128):
                    lanes = slice(128 * lt, 128 * (lt + 1))
                    wl = slice(B_BLK * kb + 128 * lt, B_BLK * kb + 128 * (lt + 1))
                    win = ubuf_s[pl.ds(pl.multiple_of(c * CHUNK, CHUNK), CHUNK + 32), lanes]
                    acc = jnp.broadcast_to(bcb_ref[:, wl], (CHUNK, 128))
                    for r in range(8):
                        taps = [k for k in range(B_CONV) if (k + 1) % 8 == r]
                        sh = win if r == 0 else pltpu.roll(win, CHUNK + 32 - r, axis=0)
                        for k in taps:
                            q = (k + 1) // 8
                            acc = acc + sh[8 * q:8 * q + CHUNK] * bcw_ref[k:k + 1, wl]
                    uc_s[_rows(c), wl] = acc
                return carry
            lax.fori_loop(0, nc, conv_body, 0, unroll=2)

        g = blg_ref[...]
        b = blb_ref[...]

        def ln_body(t, carry):
            u = jax.nn.silu(_layer_norm_rows(uc_s[_tile(t), :], g, b))
            ycat_s[_tile(t), A_INNER:A_INNER + B_CH] = (u * sg_s[_tile(t), :]).astype(BF16)
            return carry
        lax.fori_loop(0, L // ROW_TILE, ln_body, 0, unroll=2)

    pl.run_scoped(
        conformer,
        pltpu.VMEM((L, B_BLK), F32),
        pltpu.VMEM((L, B_BLK), F32),
        pltpu.VMEM((L + 32, B_BLK), F32),
        pltpu.VMEM((L, B_CH), F32),
        pltpu.VMEM((L, B_CH), F32),
    )

    _out_phase(L, x_ref, pos_refs, mods_ref, ycat_s, wout_ref, lng_ref, lnb_ref, out_ref)


def _const_spec(shape):
    nd = len(shape)
    return pl.BlockSpec(shape, lambda s: (0,) * nd, pipeline_mode=pl.Buffered(1))


def _l0_call(x, mods, mod_row0, pos_tabs, h0, weights, emit_states):
    nb, L, _ = x.shape
    nc = L // CHUNK
    has_pos = pos_tabs is not None
    has_h0 = h0 is not None
    row_of = (lambda s: (s + mod_row0, 0, 0)) if mod_row0 else (lambda s: (0, 0, 0))
    in_specs = [pl.BlockSpec((1, L, D_MODEL), lambda s: (s, 0, 0)),
                pl.BlockSpec((1, 1, 3 * D_MODEL), row_of)]
    args = [x, mods]
    if has_pos:
        in_specs += [_const_spec(t.shape) for t in pos_tabs]
        args += list(pos_tabs)
    st_spec = pl.BlockSpec((1, A_HEADS, A_HEAD_DIM, A_D_STATE), lambda s: (s, 0, 0, 0))
    if has_h0:
        in_specs += [st_spec, st_spec]
        args += list(h0)
    in_specs += [_const_spec(w.shape) for w in weights]
    args += list(weights)
    out_specs = [pl.BlockSpec((1, L, D_MODEL), lambda s: (s, 0, 0))]
    out_shape = [jax.ShapeDtypeStruct((nb, L, D_MODEL), F32)]
    if emit_states:
        out_specs += [st_spec, st_spec]
        out_shape += [jax.ShapeDtypeStruct((nb, A_HEADS, A_HEAD_DIM, A_D_STATE), F32)] * 2
    scratch = [pltpu.VMEM((L, D_MODEL), BF16),
               pltpu.VMEM((L, 2 * D_MODEL), BF16),
               pltpu.VMEM((L, HS_LANES), F32),
               pltpu.VMEM((L, HS_LANES), F32),
               pltpu.VMEM((nc, HS_LANES, CHUNK), F32),
               pltpu.VMEM((nc, HS_LANES, CHUNK), F32)]
    return pl.pallas_call(
        functools.partial(_l0_kernel, L, has_pos, has_h0, emit_states),
        grid=(nb,),
        in_specs=in_specs,
        out_specs=out_specs,
        out_shape=out_shape,
        scratch_shapes=scratch,
        compiler_params=pltpu.CompilerParams(dimension_semantics=("arbitrary",), vmem_limit_bytes=VMEM_LIMIT),
        name="layer0_L%d" % L,
    )(*args)


def _l1_kernel(L, *refs):
    nc = L // CHUNK
    (x_ref, mods_ref, wu_ref, wv_ref, wg_ref, cg_ref, cb_ref, ws_ref, bs_ref,
     wout_ref, lng_ref, lnb_ref, out_ref, h_s, ycat_s, vn_s) = refs
    _modulate_phase(L, x_ref, None, mods_ref, h_s)

    def v_phase(pv_s):
        g = cg_ref[...]
        b = cb_ref[...]
        rb = min(L, OUT_BLOCK)
        half = ROW_TILE // 2
        for r0 in range(0, L, rb):
            pv_s[...] = _dot(h_s[pl.ds(r0, rb), :], wv_ref[...])

            def ln_body(t, carry):
                src = pl.ds(pl.multiple_of(t * half, half), half)
                dst = pl.ds(pl.multiple_of(r0 + t * half, half), half)
                vn_s[dst, :] = _layer_norm_rows(pv_s[src, :], g, b).astype(BF16)
                return carry
            lax.fori_loop(0, rb // half, ln_body, 0, unroll=2)
    pl.run_scoped(v_phase, pltpu.VMEM((min(L, OUT_BLOCK), C_WIDTH), F32))

    def gate_phase(pu_s, pg_s):
        cblk = 2 * C_GROUP_DIM
        for kb in range(C_WIDTH // cblk):
            cols = slice(cblk * kb, cblk * (kb + 1))
            pu_s[...] = _dot(h_s[...], wu_ref[:, cols])
            pg_s[...] = _dot(h_s[...], wg_ref[:, cols])

            def body(c, carry):
                parts = []
                for q in range(2):
                    grp = 2 * kb + q
                    vg = vn_s[_rows(c), C_GROUP_DIM * grp:C_GROUP_DIM * (grp + 1)]
                    parts.append(_dot(ws_ref[grp], vg))
                s = jnp.concatenate(parts, axis=1) + bs_ref[:, cols]
                ycat_s[_rows(c), cols] = (pu_s[_rows(c), :] * s * jax.nn.silu(pg_s[_rows(c), :])).astype(BF16)
                return carry
            lax.fori_loop(0, nc, body, 0, unroll=2)
    pl.run_scoped(gate_phase, pltpu.VMEM((L, 2 * C_GROUP_DIM), F32), pltpu.VMEM((L, 2 * C_GROUP_DIM), F32))

    _out_phase(L, x_ref, None, mods_ref, ycat_s, wout_ref, lng_ref, lnb_ref, out_ref)


def _l1_call(x, mods, mod_row0, weights):
    nb, L, _ = x.shape
    row_of = (lambda s: (s + mod_row0, 0, 0)) if mod_row0 else (lambda s: (0, 0, 0))
    in_specs = [pl.BlockSpec((1, L, D_MODEL), lambda s: (s, 0, 0)),
                pl.BlockSpec((1, 1, 3 * D_MODEL), row_of)]
    in_specs += [_const_spec(w.shape) for w in weights]
    scratch = [pltpu.VMEM((L, D_MODEL), BF16),
               pltpu.VMEM((L, C_WIDTH), BF16),
               pltpu.VMEM((L, C_WIDTH), BF16)]
    return pl.pallas_call(
        functools.partial(_l1_kernel, L),
        grid=(nb,),
        in_specs=in_specs,
        out_specs=pl.BlockSpec((1, L, D_MODEL), lambda s: (s, 0, 0)),
        out_shape=jax.ShapeDtypeStruct((nb, L, D_MODEL), F32),
        scratch_shapes=scratch,
        compiler_params=pltpu.CompilerParams(dimension_semantics=("arbitrary",), vmem_limit_bytes=VMEM_LIMIT),
        name="layer1_L%d" % L,
    )(x, mods, *weights)


def _pos_tables(L):
    rows = L // GRID_W
    quarter = D_MODEL // 4
    freqs = 1.0 / (POS_BASE ** (jnp.arange(quarter, dtype=F32) / quarter))
    r = jnp.arange(rows, dtype=F32)[:, None] * freqs
    cc = jnp.arange(GRID_W, dtype=F32)[:, None] * freqs
    emb_r = jnp.concatenate([jnp.sin(r), jnp.cos(r)], axis=-1)
    emb_c = jnp.concatenate([jnp.sin(cc), jnp.cos(cc)], axis=-1)
    return jnp.broadcast_to(emb_r[:, None, :], (rows, 8, D_MODEL // 2)), emb_c


def _hs_row(f, b):
    return jnp.pad(jnp.concatenate([f, b]).astype(F32), (0, HS_LANES - 2 * A_HEADS)).reshape(1, HS_LANES)


def kernel(x_prompt, x_sample, state_ssd_fwd_l0, state_ssd_bwd_l0, c, c_ctx, ada_w_l0, ada_b_l0, w_in_l0, a_conv_w_l0, a_conv_b_l0, a_dt_bias_f_l0, a_dt_bias_b_l0, a_log_f_l0, a_log_b_l0, a_d_l0, a_norm_w_l0, b_conv_w_l0, b_conv_b_l0, b_ln_g_l0, b_ln_b_l0, w_out_l0, ln_g_l0, ln_b_l0, ada_w_l1, ada_b_l1, w_in_l1, c_ln_g_l1, c_ln_b_l1, c_ws_l1, c_bs_l1, w_out_l1, ln_g_l1, ln_b_l1):
    dec_batch = c.shape[0]
    cond = jnp.concatenate([c_ctx[None, :], c, jnp.zeros((ADA_ROWS - 1 - dec_batch, D_MODEL), F32)], axis=0)
    mods0, mods1 = _ada_call(cond, ada_w_l0, ada_b_l0, ada_w_l1, ada_b_l1)
    mods0 = mods0.reshape(ADA_ROWS, 1, 3 * D_MODEL)
    mods1 = mods1.reshape(ADA_ROWS, 1, 3 * D_MODEL)

    a_in = A_INNER + A_INNER + 2 * A_GN
    row = lambda v: v.astype(F32).reshape(1, -1)
    w0 = [
        w_in_l0[:, :a_in].astype(BF16),
        jnp.pad(w_in_l0[:, a_in:a_in + 2 * A_HEADS], ((0, 0), (0, HS_LANES - 2 * A_HEADS))).astype(BF16),
        w_in_l0[:, a_in + 2 * A_HEADS:].astype(BF16),
        a_conv_w_l0.astype(F32), row(a_conv_b_l0),
        _hs_row(a_dt_bias_f_l0, a_dt_bias_b_l0), _hs_row(a_log_f_l0, a_log_b_l0),
        row(jnp.repeat(a_d_l0, A_HEAD_DIM)), row(a_norm_w_l0),
        jnp.pad(b_conv_w_l0.astype(F32), ((0, 1), (0, 0))), row(b_conv_b_l0), row(b_ln_g_l0), row(b_ln_b_l0),
        w_out_l0.astype(BF16), row(ln_g_l0), row(ln_b_l0),
    ]
    pos_tabs = _pos_tables(x_sample.shape[1])
    xp1, hf, hb = _l0_call(x_prompt, mods0, 0, None, None, w0, True)
    (xs1,) = _l0_call(x_sample, mods0, 1, pos_tabs, (state_ssd_fwd_l0, state_ssd_bwd_l0), w0, False)

    w1 = [
        w_in_l1[:, :C_WIDTH].astype(BF16), w_in_l1[:, C_WIDTH:2 * C_WIDTH].astype(BF16),
        w_in_l1[:, 2 * C_WIDTH:].astype(BF16),
        row(c_ln_g_l1), row(c_ln_b_l1), c_ws_l1.astype(BF16),
        jnp.repeat(c_bs_l1.astype(F32).T, C_GROUP_DIM, axis=1),
        w_out_l1.astype(BF16), row(ln_g_l1), row(ln_b_l1),
    ]
    y_prompt = _l1_call(xp1, mods1, 0, w1)
    y_sample = _l1_call(xs1, mods1, 1, w1)
    return (y_prompt, y_sample, hf, hb)
```

```python
import functools

import jax
import jax.numpy as jnp
from jax import lax
from jax.experimental import pallas as pl
from jax.experimental.pallas import tpu as pltpu

F32 = jnp.float32
BF16 = jnp.bfloat16
HIGHEST = lax.Precision.HIGHEST

D_MODEL = 1024
GRID_W = 64
POS_BASE = 10000.0
A_INNER = 1024
A_HEAD_DIM = 64
A_HEADS = 16
A_GROUPS = 4
A_HPG = 4
A_D_STATE = 128
A_GN = 512
A_CONV = 4
A_GROUP_W = A_HPG * A_HEAD_DIM
B_CH = 1024
B_CONV = 31
B_BLK = 256
B_BLOCKS = B_CH // B_BLK
C_WIDTH = 2048
C_GROUPS = 16
C_GROUP_DIM = 128
CHUNK = 128
ROW_TILE = 32
OUT_BLOCK = 512
PROJ_W = 3 * B_BLK
PROJ_HALO = 8
DEPTH = 2
ALPHA = (2 * DEPTH) ** 0.25
LN_EPS = 1e-5
RMS_EPS = 1e-5
HS_LANES = 128
W0_DT = 2 * A_INNER + 2 * A_GN
W0_B = W0_DT + HS_LANES
ADA_ROWS = 16
VMEM_LIMIT = 58 * 1024 * 1024


def _dot(a, b):
    return jnp.dot(a, b, preferred_element_type=F32)


def _dot_nt(a, b):
    return lax.dot_general(a, b, (((1,), (1,)), ((), ())), preferred_element_type=F32)


def _layer_norm_rows(t, g, b):
    mu = jnp.mean(t, axis=-1, keepdims=True)
    tc = t - mu
    var = jnp.mean(tc * tc, axis=-1, keepdims=True)
    return tc * lax.rsqrt(var + LN_EPS) * g + b


def _expand(v, e):
    hi = v.astype(BF16)
    lo = (v - hi.astype(F32)).astype(BF16)
    return _dot(hi, e) + _dot(lo, e)


def _rows(c):
    return pl.ds(pl.multiple_of(c * CHUNK, CHUNK), CHUNK)


def _tile(t):
    return pl.ds(pl.multiple_of(t * ROW_TILE, ROW_TILE), ROW_TILE)


def _ada_kernel(cond_ref, w0_ref, b0_ref, w1_ref, b1_ref, o0_ref, o1_ref):
    s = jax.nn.silu(cond_ref[...])
    o0_ref[...] = jnp.dot(s, w0_ref[...], preferred_element_type=F32, precision=HIGHEST) + b0_ref[...]
    o1_ref[...] = jnp.dot(s, w1_ref[...], preferred_element_type=F32, precision=HIGHEST) + b1_ref[...]


def _ada_call(cond, w0, b0, w1, b1):
    bn = 768
    n = 3 * D_MODEL
    wspec = pl.BlockSpec((D_MODEL, bn), lambda j: (0, j))
    bspec = pl.BlockSpec((1, bn), lambda j: (0, j))
    ospec = pl.BlockSpec((ADA_ROWS, bn), lambda j: (0, j))
    return pl.pallas_call(
        _ada_kernel,
        grid=(n // bn,),
        in_specs=[pl.BlockSpec((ADA_ROWS, D_MODEL), lambda j: (0, 0)), wspec, bspec, wspec, bspec],
        out_specs=[ospec, ospec],
        out_shape=[jax.ShapeDtypeStruct((ADA_ROWS, n), F32)] * 2,
        compiler_params=pltpu.CompilerParams(dimension_semantics=("arbitrary",)),
        name="adaln",
    )(cond, w0, b0.reshape(1, n), w1, b1.reshape(1, n))


def _x_tile(x_ref, pos_refs, t):
    xt = x_ref[0, _tile(t), :]
    if pos_refs is not None:
        er_ref, ec_ref = pos_refs
        tiles_per_grid_row = GRID_W // ROW_TILE
        grid_row = lax.div(t, tiles_per_grid_row)
        col0 = pl.multiple_of(lax.rem(t, tiles_per_grid_row) * ROW_TILE, ROW_TILE)
        left = jnp.tile(er_ref[grid_row], (ROW_TILE // 8, 1))
        xt = xt + jnp.concatenate([left, ec_ref[pl.ds(col0, ROW_TILE), :]], axis=1)
    return xt


def _modulate_chunk(c, x_ref, pos_refs, shift, scale1, h_s):
    for j in range(CHUNK // ROW_TILE):
        t = c * (CHUNK // ROW_TILE) + j
        h_s[_tile(t), :] = (_x_tile(x_ref, pos_refs, t) * scale1 + shift).astype(BF16)


def _out_phase(L, x_ref, pos_refs, mods_ref, ycat_s, wout_ref, lng_ref, lnb_ref, out_ref):
    gate = mods_ref[0, :, 2 * D_MODEL:3 * D_MODEL]
    g = lng_ref[...]
    b = lnb_ref[...]
    rb = min(L, OUT_BLOCK)
    tiles = CHUNK // ROW_TILE

    def run(y_s):
        for r0 in range(0, L, rb):
            y_s[...] = _dot(ycat_s[pl.ds(r0, rb), :], wout_ref[...])

            def body(c, carry):
                for j in range(tiles):
                    t = c * tiles + j
                    tt = t + r0 // ROW_TILE
                    v = ALPHA * _x_tile(x_ref, pos_refs, tt) + gate * y_s[_tile(t), :]
                    out_ref[0, _tile(tt), :] = _layer_norm_rows(v, g, b)
                return carry
            lax.fori_loop(0, rb // CHUNK, body, 0, unroll=2)
    pl.run_scoped(run, pltpu.VMEM((rb, D_MODEL), F32))


def _ssd_pieces(w_ref, g):
    return [(w_ref, A_GROUP_W * g, 0, A_GROUP_W),
            (w_ref, A_INNER + A_GROUP_W * g, A_GROUP_W, A_GROUP_W),
            (w_ref, 2 * A_INNER + A_D_STATE * g, 2 * A_GROUP_W, A_D_STATE),
            (w_ref, 2 * A_INNER + A_GN + A_D_STATE * g, 2 * A_GROUP_W + A_D_STATE, A_D_STATE)]


def _conf_pieces(w_ref, kb):
    return [(w_ref, W0_B + B_BLK * kb, 0, B_BLK),
            (w_ref, W0_B + B_CH + B_BLK * kb, B_BLK, B_BLK)]


def _l0_kernel(L, has_pos, has_h0, emit_states, *refs):
    nc = L // CHUNK
    it = iter(refs)
    x_ref = next(it)
    mods_ref = next(it)
    pos_refs = (next(it), next(it)) if has_pos else None
    h0_refs = (next(it), next(it)) if has_h0 else None
    (win_ref, acw_ref, acb_ref, dtb_ref, alog_ref, dexp_ref, nw_ref,
     bcw_ref, bcb_ref, blg_ref, blb_ref, wout_ref, lng_ref, lnb_ref) = [next(it) for _ in range(14)]
    out_ref = next(it)
    st_refs = (next(it), next(it)) if emit_states else None
    h_s, ycat_s, dt_s, acs_s, dtT_s, acsT_s, proj_s = [next(it) for _ in range(7)]

    def project_all(pieces):
        for w_ref, wcol, pcol, width in pieces:
            proj_s[pl.ds(PROJ_HALO, L), pcol:pcol + width] = _dot(h_s[...], w_ref[:, wcol:wcol + width])

    stage_pieces = ([_ssd_pieces(win_ref, g) for g in range(A_GROUPS)]
                    + [_conf_pieces(win_ref, kb) for kb in range(B_BLOCKS)])

    proj_s[pl.ds(0, PROJ_HALO), :] = jnp.zeros((PROJ_HALO, PROJ_W), F32)
    proj_s[pl.ds(PROJ_HALO + L, PROJ_HALO), :] = jnp.zeros((PROJ_HALO, PROJ_W), F32)

    shift = mods_ref[0, :, 0:D_MODEL]
    scale1 = 1.0 + mods_ref[0, :, D_MODEL:2 * D_MODEL]

    def head_body(c, carry):
        _modulate_chunk(c, x_ref, pos_refs, shift, scale1, h_s)
        return carry
    lax.fori_loop(0, nc, head_body, 0, unroll=2)
    dt_s[...] = _dot(h_s[...], win_ref[:, W0_DT:W0_DT + HS_LANES])

    ii = lax.broadcasted_iota(jnp.int32, (CHUNK, CHUNK), 0)
    jj = lax.broadcasted_iota(jnp.int32, (CHUNK, CHUNK), 1)
    lane_hs = lax.broadcasted_iota(jnp.int32, (1, HS_LANES), 1)
    is_fwd_lane = lane_hs < A_HEADS

    tril = (ii >= jj).astype(F32)
    triu = (ii <= jj).astype(F32)
    a_neg = -jnp.exp(alog_ref[...])

    def dt_body(c, carry):
        dt = jax.nn.softplus(dt_s[_rows(c), :] + dtb_ref[...])
        dt_s[_rows(c), :] = dt
        adt = dt * a_neg
        cum_f = jnp.dot(tril, adt, preferred_element_type=F32, precision=HIGHEST)
        cum_b = jnp.dot(triu, adt, preferred_element_type=F32, precision=HIGHEST)
        acs = jnp.where(is_fwd_lane, cum_f, cum_b)
        acs_s[_rows(c), :] = acs
        acsT_s[c] = acs.T
        dtT_s[c] = dt.T
        return carry
    lax.fori_loop(0, nc, dt_body, 0, unroll=2)

    lane_g = lax.broadcasted_iota(jnp.int32, (CHUNK, A_GROUP_W), 1) // A_HEAD_DIM
    exp_row = lax.broadcasted_iota(jnp.int32, (HS_LANES, 2 * A_GROUP_W), 0)
    exp_col = lax.broadcasted_iota(jnp.int32, (HS_LANES, 2 * A_GROUP_W), 1)
    exp_src = exp_col // A_HEAD_DIM + jnp.where(exp_col < A_GROUP_W, 0, A_HEADS - A_HPG)

    def ssd_group(g, xs_s, cb_s, c_s, st_s, cd_s):
        project_all(stage_pieces[g])
        e_fb = (exp_row == A_HPG * g + exp_src).astype(BF16)
        zc = slice(A_GROUP_W * g, A_GROUP_W * (g + 1))

        def conv_cols(ref):
            x0 = A_GROUP_W * g
            b0 = A_INNER + A_D_STATE * g
            c0 = A_INNER + A_GN + A_D_STATE * g
            return jnp.concatenate([ref[:, x0:x0 + A_GROUP_W], ref[:, b0:b0 + A_D_STATE],
                                    ref[:, c0:c0 + A_D_STATE]], axis=1)
        cw = conv_cols(acw_ref)
        cbias = conv_cols(acb_ref)

        def chunk_scalars(c):
            acs = acs_s[_rows(c), :]
            dt = dt_s[_rows(c), :]
            ref_row = jnp.where(is_fwd_lane, acs[CHUNK - 1:CHUNK, :], acs[0:1, :])
            decay_end = jnp.exp(jnp.minimum(ref_row - acs, 0.0))
            chunk_decay = jnp.broadcast_to(jnp.exp(jnp.minimum(ref_row, 0.0)), (16, HS_LANES))
            return decay_end * dt, chunk_decay

        def conv_body(c, carry):
            win = proj_s[pl.ds(pl.multiple_of(c * CHUNK, CHUNK), CHUNK + 2 * PROJ_HALO), A_GROUP_W:]
            acc = cbias + win[6:6 + CHUNK] * cw[0:1]
            for k in range(1, A_CONV):
                acc = acc + win[6 + k:6 + k + CHUNK] * cw[k:k + 1]
            act = jax.nn.silu(acc)
            xs = act[:, 0:A_GROUP_W]
            bm = act[:, A_GROUP_W:A_GROUP_W + A_D_STATE]
            xs_s[_rows(c), :] = xs
            cm = act[:, A_GROUP_W + A_D_STATE:].astype(BF16)
            c_s[_rows(c), :] = cm
            cb_s[c] = _dot_nt(cm, bm.astype(BF16))
            w, chunk_decay = chunk_scalars(c)
            wexp = _expand(w, e_fb)
            xw = jnp.concatenate([xs * wexp[:, 0:A_GROUP_W], xs * wexp[:, A_GROUP_W:]], axis=1)
            st_s[c] = _dot(bm.T.astype(BF16), xw.astype(BF16))
            cd_s[c] = _expand(chunk_decay, e_fb)
            return carry
        lax.fori_loop(0, nc, conv_body, 0, unroll=2)

        if has_h0:
            hf0 = h0_refs[0][0, A_HPG * g:A_HPG * (g + 1)].reshape(A_GROUP_W, A_D_STATE).T
            hb0 = h0_refs[1][0, A_HPG * g:A_HPG * (g + 1)].reshape(A_GROUP_W, A_D_STATE).T
        else:
            hf0 = jnp.zeros((A_D_STATE, A_GROUP_W), F32)
            hb0 = hf0

        def scan_body(t, carry):
            hf, hb = carry
            tb = nc - 1 - t
            sf = st_s[t, :, 0:A_GROUP_W]
            st_s[t, :, 0:A_GROUP_W] = hf
            hf = hf * cd_s[t, 0:1, 0:A_GROUP_W] + sf
            sb = st_s[tb, :, A_GROUP_W:]
            st_s[tb, :, A_GROUP_W:] = hb
            hb = hb * cd_s[tb, 0:1, A_GROUP_W:] + sb
            return hf, hb
        hf_fin, hb_fin = lax.fori_loop(0, nc, scan_body, (hf0, hb0))
        if emit_states:
            st_refs[0][0, A_HPG * g:A_HPG * (g + 1)] = hf_fin.T.reshape(A_HPG, A_HEAD_DIM, A_D_STATE)
            st_refs[1][0, A_HPG * g:A_HPG * (g + 1)] = hb_fin.T.reshape(A_HPG, A_HEAD_DIM, A_D_STATE)

        d_skip = dexp_ref[:, zc]
        nw = nw_ref[:, zc]

        def out_body(c, carry):
            acs = acs_s[_rows(c), :]
            acsT = acsT_s[c]
            dtT = dtT_s[c]
            xs = xs_s[_rows(c), :]
            cmat = c_s[_rows(c), :]
            cbm = cb_s[c]
            y = d_skip * xs
            for r in range(A_HPG):
                hf = A_HPG * g + r
                hb = A_HEADS + hf
                seg = jnp.where(ii >= jj, acs[:, hf:hf + 1] - acsT[hf:hf + 1, :],
                                acs[:, hb:hb + 1] - acsT[hb:hb + 1, :])
                dtf = dtT[hf:hf + 1, :]
                dtb = dtT[hb:hb + 1, :]
                wd = jnp.where(ii > jj, dtf, jnp.where(ii < jj, dtb, dtf + dtb))
                m = (cbm * jnp.exp(seg) * wd).astype(BF16)
                y = y + _dot(m, jnp.where(lane_g == r, xs, 0.0).astype(BF16))
            e_in = _expand(jnp.exp(jnp.minimum(acs, 0.0)), e_fb)
            off = e_in * _dot(cmat, st_s[c].astype(BF16))
            y = y + off[:, 0:A_GROUP_W] + off[:, A_GROUP_W:]
            pz = proj_s[pl.ds(pl.multiple_of(c * CHUNK + PROJ_HALO, 8), CHUNK), 0:A_GROUP_W]
            y = y * jax.nn.silu(pz)
            y = y * lax.rsqrt(jnp.mean(y * y, axis=-1, keepdims=True) + RMS_EPS) * nw
            ycat_s[_rows(c), zc] = y.astype(BF16)
            return carry
        lax.fori_loop(0, nc, out_body, 0, unroll=2)

    def ssd_all(*scratch):
        for g in range(A_GROUPS):
            ssd_group(g, *scratch)
    pl.run_scoped(
        ssd_all,
        pltpu.VMEM((L, A_GROUP_W), F32),
        pltpu.VMEM((nc, CHUNK, CHUNK), F32),
        pltpu.VMEM((L, A_D_STATE), BF16),
        pltpu.VMEM((nc, A_D_STATE, 2 * A_GROUP_W), F32),
        pltpu.VMEM((nc, 16, 2 * A_GROUP_W), F32),
    )

    def conformer(ubuf_s, uc_s):
        ubuf_s[pl.ds(0, 16), :] = jnp.zeros((16, B_BLK), F32)
        ubuf_s[pl.ds(16 + L, 16), :] = jnp.zeros((16, B_BLK), F32)
        for kb in range(B_BLOCKS):
            project_all(stage_pieces[A_GROUPS + kb])

            def glu_body(c, carry):
                src = pl.ds(pl.multiple_of(c * CHUNK + PROJ_HALO, 8), CHUNK)
                ubuf_s[pl.ds(pl.multiple_of(16 + c * CHUNK, 8), CHUNK), :] = (
                    proj_s[src, 0:B_BLK] * jax.nn.sigmoid(proj_s[src, B_BLK:2 * B_BLK]))
                return carry
            lax.fori_loop(0, nc, glu_body, 0, unroll=2)

            def conv_body(c, carry):
                for lt in range(B_BLK // 128):
                    lanes = slice(128 * lt, 128 * (lt + 1))
                    wl = slice(B_BLK * kb + 128 * lt, B_BLK * kb + 128 * (lt + 1))
                    win = ubuf_s[pl.ds(pl.multiple_of(c * CHUNK, CHUNK), CHUNK + 32), lanes]
                    acc = jnp.broadcast_to(bcb_ref[:, wl], (CHUNK, 128))
                    for r in range(8):
                        sh = win if r == 0 else pltpu.roll(win, CHUNK + 32 - r, axis=0)
                        for k in [k for k in range(B_CONV) if (k + 1) % 8 == r]:
                            q = (k + 1) // 8
                            acc = acc + sh[8 * q:8 * q + CHUNK] * bcw_ref[k:k + 1, wl]
                    uc_s[_rows(c), wl] = acc
                return carry
            lax.fori_loop(0, nc, conv_body, 0, unroll=2)

        g = blg_ref[...]
        b = blb_ref[...]

        def ln_body(c, carry):
            sgate = _dot(h_s[_rows(c), :], win_ref[:, W0_B + 2 * B_CH:W0_B + 3 * B_CH])
            for j in range(CHUNK // ROW_TILE):
                t = c * (CHUNK // ROW_TILE) + j
                u = jax.nn.silu(_layer_norm_rows(uc_s[_tile(t), :], g, b))
                sg = jax.nn.silu(sgate[ROW_TILE * j:ROW_TILE * (j + 1)])
                ycat_s[_tile(t), A_INNER:A_INNER + B_CH] = (u * sg).astype(BF16)
            return carry
        lax.fori_loop(0, nc, ln_body, 0, unroll=2)

    pl.run_scoped(
        conformer,
        pltpu.VMEM((L + 32, B_BLK), F32),
        pltpu.VMEM((L, B_CH), F32),
    )

    _out_phase(L, x_ref, pos_refs, mods_ref, ycat_s, wout_ref, lng_ref, lnb_ref, out_ref)


def _const_spec(shape):
    nd = len(shape)
    return pl.BlockSpec(shape, lambda s: (0,) * nd, pipeline_mode=pl.Buffered(1))


def _l0_call(x, mods, mod_row0, pos_tabs, h0, weights, emit_states):
    nb, L, _ = x.shape
    nc = L // CHUNK
    has_pos = pos_tabs is not None
    has_h0 = h0 is not None
    row_of = (lambda s: (s + mod_row0, 0, 0)) if mod_row0 else (lambda s: (0, 0, 0))
    in_specs = [pl.BlockSpec((1, L, D_MODEL), lambda s: (s, 0, 0)),
                pl.BlockSpec((1, 1, 3 * D_MODEL), row_of)]
    args = [x, mods]
    if has_pos:
        in_specs += [_const_spec(t.shape) for t in pos_tabs]
        args += list(pos_tabs)
    st_spec = pl.BlockSpec((1, A_HEADS, A_HEAD_DIM, A_D_STATE), lambda s: (s, 0, 0, 0))
    if has_h0:
        in_specs += [st_spec, st_spec]
        args += list(h0)
    in_specs += [_const_spec(w.shape) for w in weights]
    args += list(weights)
    out_specs = [pl.BlockSpec((1, L, D_MODEL), lambda s: (s, 0, 0))]
    out_shape = [jax.ShapeDtypeStruct((nb, L, D_MODEL), F32)]
    if emit_states:
        out_specs += [st_spec, st_spec]
        out_shape += [jax.ShapeDtypeStruct((nb, A_HEADS, A_HEAD_DIM, A_D_STATE), F32)] * 2
    scratch = [pltpu.VMEM((L, D_MODEL), BF16),
               pltpu.VMEM((L, 2 * D_MODEL), BF16),
               pltpu.VMEM((L, HS_LANES), F32),
               pltpu.VMEM((L, HS_LANES), F32),
               pltpu.VMEM((nc, HS_LANES, CHUNK), F32),
               pltpu.VMEM((nc, HS_LANES, CHUNK), F32),
               pltpu.VMEM((L + 2 * PROJ_HALO, PROJ_W), F32)]
    return pl.pallas_call(
        functools.partial(_l0_kernel, L, has_pos, has_h0, emit_states),
        grid=(nb,),
        in_specs=in_specs,
        out_specs=out_specs,
        out_shape=out_shape,
        scratch_shapes=scratch,
        compiler_params=pltpu.CompilerParams(dimension_semantics=("arbitrary",), vmem_limit_bytes=VMEM_LIMIT),
        name="layer0_L%d" % L,
    )(*args)


def _l1_kernel(L, *refs):
    nc = L // CHUNK
    (x_ref, mods_ref, win_ref, cg_ref, cb_ref, ws_ref, bs_ref,
     wout_ref, lng_ref, lnb_ref, out_ref, h_s, ycat_s, vn_s) = refs
    shift = mods_ref[0, :, 0:D_MODEL]
    scale1 = 1.0 + mods_ref[0, :, D_MODEL:2 * D_MODEL]

    def mod_body(c, carry):
        _modulate_chunk(c, x_ref, None, shift, scale1, h_s)
        return carry
    lax.fori_loop(0, nc, mod_body, 0, unroll=2)

    def v_phase(pv_s):
        g = cg_ref[...]
        b = cb_ref[...]
        rb = min(L, OUT_BLOCK)
        half = ROW_TILE // 2
        for r0 in range(0, L, rb):
            pv_s[...] = _dot(h_s[pl.ds(r0, rb), :], win_ref[:, C_WIDTH:2 * C_WIDTH])

            def ln_body(c, carry):
                for j in range(CHUNK // half):
                    src = pl.ds(pl.multiple_of(c * CHUNK + j * half, half), half)
                    dst = pl.ds(pl.multiple_of(r0 + c * CHUNK + j * half, half), half)
                    vn_s[dst, :] = _layer_norm_rows(pv_s[src, :], g, b).astype(BF16)
                return carry
            lax.fori_loop(0, rb // CHUNK, ln_body, 0)
    pl.run_scoped(v_phase, pltpu.VMEM((min(L, OUT_BLOCK), C_WIDTH), F32))

    def gate_phase(pu_s, pg_s):
        cblk = 2 * C_GROUP_DIM
        for kb in range(C_WIDTH // cblk):
            cols = slice(cblk * kb, cblk * (kb + 1))
            pu_s[...] = _dot(h_s[...], win_ref[:, cblk * kb:cblk * (kb + 1)])
            pg_s[...] = _dot(h_s[...], win_ref[:, 2 * C_WIDTH + cblk * kb:2 * C_WIDTH + cblk * (kb + 1)])

            def body(c, carry):
                parts = []
                for q in range(2):
                    grp = 2 * kb + q
                    vg = vn_s[_rows(c), C_GROUP_DIM * grp:C_GROUP_DIM * (grp + 1)]
                    parts.append(_dot(ws_ref[grp], vg))
                s = jnp.concatenate(parts, axis=1) + bs_ref[:, cols]
                ycat_s[_rows(c), cols] = (pu_s[_rows(c), :] * s * jax.nn.silu(pg_s[_rows(c), :])).astype(BF16)
                return carry
            lax.fori_loop(0, nc, body, 0, unroll=min(nc, 4))
    pl.run_scoped(gate_phase, pltpu.VMEM((L, 2 * C_GROUP_DIM), F32), pltpu.VMEM((L, 2 * C_GROUP_DIM), F32))

    _out_phase(L, x_ref, None, mods_ref, ycat_s, wout_ref, lng_ref, lnb_ref, out_ref)


def _l1_call(x, mods, mod_row0, weights):
    nb, L, _ = x.shape
    row_of = (lambda s: (s + mod_row0, 0, 0)) if mod_row0 else (lambda s: (0, 0, 0))
    in_specs = [pl.BlockSpec((1, L, D_MODEL), lambda s: (s, 0, 0)),
                pl.BlockSpec((1, 1, 3 * D_MODEL), row_of)]
    in_specs += [_const_spec(w.shape) for w in weights]
    scratch = [pltpu.VMEM((L, D_MODEL), BF16),
               pltpu.VMEM((L, C_WIDTH), BF16),
               pltpu.VMEM((L, C_WIDTH), BF16)]
    return pl.pallas_call(
        functools.partial(_l1_kernel, L),
        grid=(nb,),
        in_specs=in_specs,
        out_specs=pl.BlockSpec((1, L, D_MODEL), lambda s: (s, 0, 0)),
        out_shape=jax.ShapeDtypeStruct((nb, L, D_MODEL), F32),
        scratch_shapes=scratch,
        compiler_params=pltpu.CompilerParams(dimension_semantics=("arbitrary",), vmem_limit_bytes=VMEM_LIMIT),
        name="layer1_L%d" % L,
    )(x, mods, *weights)


def _pos_tables(L):
    rows = L // GRID_W
    quarter = D_MODEL // 4
    freqs = 1.0 / (POS_BASE ** (jnp.arange(quarter, dtype=F32) / quarter))
    r = jnp.arange(rows, dtype=F32)[:, None] * freqs
    cc = jnp.arange(GRID_W, dtype=F32)[:, None] * freqs
    emb_r = jnp.concatenate([jnp.sin(r), jnp.cos(r)], axis=-1)
    emb_c = jnp.concatenate([jnp.sin(cc), jnp.cos(cc)], axis=-1)
    return jnp.broadcast_to(emb_r[:, None, :], (rows, 8, D_MODEL // 2)), emb_c


def _hs_row(f, b):
    return jnp.pad(jnp.concatenate([f, b]).astype(F32), (0, HS_LANES - 2 * A_HEADS)).reshape(1, HS_LANES)


def kernel(x_prompt, x_sample, state_ssd_fwd_l0, state_ssd_bwd_l0, c, c_ctx, ada_w_l0, ada_b_l0, w_in_l0, a_conv_w_l0, a_conv_b_l0, a_dt_bias_f_l0, a_dt_bias_b_l0, a_log_f_l0, a_log_b_l0, a_d_l0, a_norm_w_l0, b_conv_w_l0, b_conv_b_l0, b_ln_g_l0, b_ln_b_l0, w_out_l0, ln_g_l0, ln_b_l0, ada_w_l1, ada_b_l1, w_in_l1, c_ln_g_l1, c_ln_b_l1, c_ws_l1, c_bs_l1, w_out_l1, ln_g_l1, ln_b_l1):
    dec_batch = c.shape[0]
    cond = jnp.concatenate([c_ctx[None, :], c, jnp.zeros((ADA_ROWS - 1 - dec_batch, D_MODEL), F32)], axis=0)
    mods0, mods1 = _ada_call(cond, ada_w_l0, ada_b_l0, ada_w_l1, ada_b_l1)
    mods0 = mods0.reshape(ADA_ROWS, 1, 3 * D_MODEL)
    mods1 = mods1.reshape(ADA_ROWS, 1, 3 * D_MODEL)

    a_in = A_INNER + A_INNER + 2 * A_GN
    row = lambda v: v.astype(F32).reshape(1, -1)
    w0 = [
        jnp.concatenate([w_in_l0[:, :a_in],
                         jnp.pad(w_in_l0[:, a_in:a_in + 2 * A_HEADS], ((0, 0), (0, HS_LANES - 2 * A_HEADS))),
                         w_in_l0[:, a_in + 2 * A_HEADS:]], axis=1).astype(BF16),
        a_conv_w_l0.astype(F32), row(a_conv_b_l0),
        _hs_row(a_dt_bias_f_l0, a_dt_bias_b_l0), _hs_row(a_log_f_l0, a_log_b_l0),
        row(jnp.repeat(a_d_l0, A_HEAD_DIM)), row(a_norm_w_l0),
        jnp.pad(b_conv_w_l0.astype(F32), ((0, 1), (0, 0))), row(b_conv_b_l0), row(b_ln_g_l0), row(b_ln_b_l0),
        w_out_l0.astype(BF16), row(ln_g_l0), row(ln_b_l0),
    ]
    pos_tabs = _pos_tables(x_sample.shape[1])
    xp1, hf, hb = _l0_call(x_prompt, mods0, 0, None, None, w0, True)
    (xs1,) = _l0_call(x_sample, mods0, 1, pos_tabs, (state_ssd_fwd_l0, state_ssd_bwd_l0), w0, False)

    w1 = [
        w_in_l1.astype(BF16),
        row(c_ln_g_l1), row(c_ln_b_l1), c_ws_l1.astype(BF16),
        jnp.repeat(c_bs_l1.astype(F32).T, C_GROUP_DIM, axis=1),
        w_out_l1.astype(BF16), row(ln_g_l1), row(ln_b_l1),
    ]
    y_prompt = _l1_call(xp1, mods1, 0, w1)
    y_sample = _l1_call(xs1, mods1, 1, w1)
    return (y_prompt, y_sample, hf, hb)
```

```python
import functools

import jax
import jax.numpy as jnp
from jax import lax
from jax.experimental import pallas as pl
from jax.experimental.pallas import tpu as pltpu

F32 = jnp.float32
BF16 = jnp.bfloat16
HIGHEST = lax.Precision.HIGHEST

D_MODEL = 1024
GRID_W = 64
POS_BASE = 10000.0
A_INNER = 1024
A_HEAD_DIM = 64
A_HEADS = 16
A_GROUPS = 4
A_HPG = 4
A_D_STATE = 128
A_GN = 512
A_CONV = 4
A_GROUP_W = A_HPG * A_HEAD_DIM
B_CH = 1024
B_CONV = 31
B_BLK = 256
B_BLOCKS = B_CH // B_BLK
C_WIDTH = 2048
C_GROUPS = 16
C_GROUP_DIM = 128
CHUNK = 128
ROW_TILE = 32
OUT_BLOCK = 512
HOST_UNROLL = 4
PROJ_W = 3 * B_BLK
PROJ_HALO = 8
DEPTH = 2
ALPHA = (2 * DEPTH) ** 0.25
LN_EPS = 1e-5
RMS_EPS = 1e-5
HS_LANES = 128
W0_DT = 2 * A_INNER + 2 * A_GN
W0_B = W0_DT + HS_LANES
ADA_ROWS = 16
VMEM_LIMIT = 58 * 1024 * 1024


def _dot(a, b):
    return jnp.dot(a, b, preferred_element_type=F32)


def _dot_nt(a, b):
    return lax.dot_general(a, b, (((1,), (1,)), ((), ())), preferred_element_type=F32)


def _layer_norm_rows(t, g, b):
    mu = jnp.mean(t, axis=-1, keepdims=True)
    tc = t - mu
    var = jnp.mean(tc * tc, axis=-1, keepdims=True)
    return tc * lax.rsqrt(var + LN_EPS) * g + b


def _expand(v, e):
    hi = v.astype(BF16)
    lo = (v - hi.astype(F32)).astype(BF16)
    return _dot(hi, e) + _dot(lo, e)


def _rows(c):
    if isinstance(c, int):
        return pl.ds(c * CHUNK, CHUNK)
    return pl.ds(pl.multiple_of(c * CHUNK, CHUNK), CHUNK)


def _tile(t):
    if isinstance(t, int):
        return pl.ds(t * ROW_TILE, ROW_TILE)
    return pl.ds(pl.multiple_of(t * ROW_TILE, ROW_TILE), ROW_TILE)


def _ada_kernel(cond_ref, w0_ref, b0_ref, w1_ref, b1_ref, o0_ref, o1_ref):
    s = jax.nn.silu(cond_ref[...])
    o0_ref[...] = jnp.dot(s, w0_ref[...], preferred_element_type=F32, precision=HIGHEST) + b0_ref[...]
    o1_ref[...] = jnp.dot(s, w1_ref[...], preferred_element_type=F32, precision=HIGHEST) + b1_ref[...]


def _ada_call(cond, w0, b0, w1, b1):
    bn = 768
    n = 3 * D_MODEL
    wspec = pl.BlockSpec((D_MODEL, bn), lambda j: (0, j))
    bspec = pl.BlockSpec((1, bn), lambda j: (0, j))
    ospec = pl.BlockSpec((ADA_ROWS, bn), lambda j: (0, j))
    return pl.pallas_call(
        _ada_kernel,
        grid=(n // bn,),
        in_specs=[pl.BlockSpec((ADA_ROWS, D_MODEL), lambda j: (0, 0)), wspec, bspec, wspec, bspec],
        out_specs=[ospec, ospec],
        out_shape=[jax.ShapeDtypeStruct((ADA_ROWS, n), F32)] * 2,
        compiler_params=pltpu.CompilerParams(dimension_semantics=("arbitrary",)),
        name="adaln",
    )(cond, w0, b0.reshape(1, n), w1, b1.reshape(1, n))


def _x_tile(x_ref, pos_refs, t):
    xt = x_ref[0, _tile(t), :]
    if pos_refs is not None:
        er_ref, ec_ref = pos_refs
        tiles_per_grid_row = GRID_W // ROW_TILE
        if isinstance(t, int):
            grid_row, col0 = t // tiles_per_grid_row, (t % tiles_per_grid_row) * ROW_TILE
        else:
            grid_row = lax.div(t, tiles_per_grid_row)
            col0 = pl.multiple_of(lax.rem(t, tiles_per_grid_row) * ROW_TILE, ROW_TILE)
        left = jnp.tile(er_ref[grid_row], (ROW_TILE // 8, 1))
        xt = xt + jnp.concatenate([left, ec_ref[pl.ds(col0, ROW_TILE), :]], axis=1)
    return xt


def _modulate_chunk(c, x_ref, pos_refs, shift, scale1, h_s):
    for j in range(CHUNK // ROW_TILE):
        t = c * (CHUNK // ROW_TILE) + j
        h_s[_tile(t), :] = (_x_tile(x_ref, pos_refs, t) * scale1 + shift).astype(BF16)


def _out_phase(L, x_ref, pos_refs, mods_ref, ycat_s, wout_ref, lng_ref, lnb_ref, out_ref, k0=0):
    gate = mods_ref[0, :, 2 * D_MODEL:3 * D_MODEL]
    g = lng_ref[...]
    b = lnb_ref[...]
    rb = min(L, OUT_BLOCK)
    nblk = L // rb
    tiles = rb // ROW_TILE

    def run(y_s):
        def project(bi):
            y_s[bi % 2] = _dot(ycat_s[pl.ds(bi * rb, rb), k0:], wout_ref[k0:, :])
        project(0)
        for bi in range(nblk):
            if bi + 1 < nblk:
                project(bi + 1)
            for j in range(tiles):
                tt = bi * tiles + j
                y = y_s[bi % 2, _tile(j), :]
                if k0:
                    y = y + out_ref[0, _tile(tt), :]
                v = ALPHA * _x_tile(x_ref, pos_refs, tt) + gate * y
                out_ref[0, _tile(tt), :] = _layer_norm_rows(v, g, b)
    pl.run_scoped(run, pltpu.VMEM((2, rb, D_MODEL), F32))


def _ssd_pieces(w_ref, g):
    return [(w_ref, A_GROUP_W * g, 0, A_GROUP_W),
            (w_ref, A_INNER + A_GROUP_W * g, A_GROUP_W, A_GROUP_W),
            (w_ref, 2 * A_INNER + A_D_STATE * g, 2 * A_GROUP_W, A_D_STATE),
            (w_ref, 2 * A_INNER + A_GN + A_D_STATE * g, 2 * A_GROUP_W + A_D_STATE, A_D_STATE)]


def _conf_pieces(w_ref, kb):
    return [(w_ref, W0_B + B_BLK * kb, 0, B_BLK),
            (w_ref, W0_B + B_CH + B_BLK * kb, B_BLK, B_BLK)]


def _l0_kernel(L, has_pos, has_h0, emit_states, *refs):
    nc = L // CHUNK
    it = iter(refs)
    x_ref = next(it)
    mods_ref = next(it)
    pos_refs = (next(it), next(it)) if has_pos else None
    h0_refs = (next(it), next(it)) if has_h0 else None
    (win_ref, acw_ref, acb_ref, dtb_ref, alog_ref, dexp_ref, nw_ref,
     bcw_ref, bcb_ref, blg_ref, blb_ref, wout_ref, lng_ref, lnb_ref) = [next(it) for _ in range(14)]
    out_ref = next(it)
    st_refs = (next(it), next(it)) if emit_states else None
    h_s, ycat_s, dt_s, acs_s, dtT_s, acsT_s, proj_s = [next(it) for _ in range(7)]

    def project_rows(buf, pieces, row0, rows):
        if pieces:
            hrows = h_s[pl.ds(row0, rows), :]
            dst0 = row0 + PROJ_HALO if isinstance(row0, int) else pl.multiple_of(row0 + PROJ_HALO, 8)
            for w_ref, wcol, pcol, width in pieces:
                proj_s[buf, pl.ds(dst0, rows), pcol:pcol + width] = _dot(hrows, w_ref[:, wcol:wcol + width])

    def chunk_loop(body, unroll=2, host=None):
        u = min(nc, unroll)

        def step(i, carry):
            if host is not None:
                project_rows(host[0], host[1], pl.multiple_of(i * (u * CHUNK), u * CHUNK), u * CHUNK)
            for j in range(u):
                body(i * u + j)
            return carry
        lax.fori_loop(0, nc // u, step, 0)

    stage_pieces = ([_ssd_pieces(win_ref, g) for g in range(A_GROUPS)]
                    + [_conf_pieces(win_ref, kb) for kb in range(B_BLOCKS)])

    for buf in range(2):
        proj_s[buf, pl.ds(0, PROJ_HALO), :] = jnp.zeros((PROJ_HALO, PROJ_W), F32)
        proj_s[buf, pl.ds(PROJ_HALO + L, PROJ_HALO), :] = jnp.zeros((PROJ_HALO, PROJ_W), F32)

    shift = mods_ref[0, :, 0:D_MODEL]
    scale1 = 1.0 + mods_ref[0, :, D_MODEL:2 * D_MODEL]

    def head_body(c, carry):
        _modulate_chunk(c, x_ref, pos_refs, shift, scale1, h_s)
        return carry
    lax.fori_loop(0, nc, head_body, 0, unroll=2)
    dt_s[...] = _dot(h_s[...], win_ref[:, W0_DT:W0_DT + HS_LANES])

    ii = lax.broadcasted_iota(jnp.int32, (CHUNK, CHUNK), 0)
    jj = lax.broadcasted_iota(jnp.int32, (CHUNK, CHUNK), 1)
    lane_hs = lax.broadcasted_iota(jnp.int32, (1, HS_LANES), 1)
    is_fwd_lane = lane_hs < A_HEADS

    tril = (ii >= jj).astype(F32)
    triu = (ii <= jj).astype(F32)
    a_neg = -jnp.exp(alog_ref[...])

    def dt_body(c):
        dt = jax.nn.softplus(dt_s[_rows(c), :] + dtb_ref[...])
        dt_s[_rows(c), :] = dt
        adt = dt * a_neg
        cum_f = jnp.dot(tril, adt, preferred_element_type=F32, precision=HIGHEST)
        cum_b = jnp.dot(triu, adt, preferred_element_type=F32, precision=HIGHEST)
        acs = jnp.where(is_fwd_lane, cum_f, cum_b)
        acs_s[_rows(c), :] = acs
        acsT_s[c] = acs.T
        dtT_s[c] = dt.T
    chunk_loop(dt_body, unroll=HOST_UNROLL, host=(0, stage_pieces[0]))

    lane_g = lax.broadcasted_iota(jnp.int32, (CHUNK, A_GROUP_W), 1) // A_HEAD_DIM
    exp_row = lax.broadcasted_iota(jnp.int32, (HS_LANES, 2 * A_GROUP_W), 0)
    exp_col = lax.broadcasted_iota(jnp.int32, (HS_LANES, 2 * A_GROUP_W), 1)
    exp_src = exp_col // A_HEAD_DIM + jnp.where(exp_col < A_GROUP_W, 0, A_HEADS - A_HPG)

    def ssd_group(g, xs_s, cb_s, c_s, st_s, cd_s):
        buf = g % 2
        nxt = stage_pieces[g + 1]
        e_fb = (exp_row == A_HPG * g + exp_src).astype(BF16)
        zc = slice(A_GROUP_W * g, A_GROUP_W * (g + 1))

        def conv_cols(ref):
            x0 = A_GROUP_W * g
            b0 = A_INNER + A_D_STATE * g
            c0 = A_INNER + A_GN + A_D_STATE * g
            return jnp.concatenate([ref[:, x0:x0 + A_GROUP_W], ref[:, b0:b0 + A_D_STATE],
                                    ref[:, c0:c0 + A_D_STATE]], axis=1)
        cw = conv_cols(acw_ref)
        cbias = conv_cols(acb_ref)

        def chunk_scalars(c):
            acs = acs_s[_rows(c), :]
            dt = dt_s[_rows(c), :]
            ref_row = jnp.where(is_fwd_lane, acs[CHUNK - 1:CHUNK, :], acs[0:1, :])
            decay_end = jnp.exp(jnp.minimum(ref_row - acs, 0.0))
            chunk_decay = jnp.broadcast_to(jnp.exp(jnp.minimum(ref_row, 0.0)), (16, HS_LANES))
            return decay_end * dt, chunk_decay

        def conv_body(c):
            win = proj_s[buf, pl.ds(pl.multiple_of(c * CHUNK, CHUNK), CHUNK + 2 * PROJ_HALO), A_GROUP_W:]
            acc = cbias
            for k in range(A_CONV):
                d = 6 + k - PROJ_HALO
                sh = win if d == 0 else pltpu.roll(win, (-d) % (CHUNK + 2 * PROJ_HALO), axis=0)
                acc = acc + sh[PROJ_HALO:PROJ_HALO + CHUNK] * cw[k:k + 1]
            act = jax.nn.silu(acc)
            xs = act[:, 0:A_GROUP_W]
            bm = act[:, A_GROUP_W:A_GROUP_W + A_D_STATE]
            xs_s[_rows(c), :] = xs
            cm = act[:, A_GROUP_W + A_D_STATE:].astype(BF16)
            c_s[_rows(c), :] = cm
            cb_s[c] = _dot_nt(cm, bm.astype(BF16))
            w, chunk_decay = chunk_scalars(c)
            wexp = _expand(w, e_fb)
            xw = jnp.concatenate([xs * wexp[:, 0:A_GROUP_W], xs * wexp[:, A_GROUP_W:]], axis=1)
            st_s[c] = _dot(bm.T.astype(BF16), xw.astype(BF16))
            cd_s[c] = _expand(chunk_decay, e_fb)
        chunk_loop(conv_body, unroll=4)

        if has_h0:
            hf0 = h0_refs[0][0, A_HPG * g:A_HPG * (g + 1)].reshape(A_GROUP_W, A_D_STATE).T
            hb0 = h0_refs[1][0, A_HPG * g:A_HPG * (g + 1)].reshape(A_GROUP_W, A_D_STATE).T
        else:
            hf0 = jnp.zeros((A_D_STATE, A_GROUP_W), F32)
            hb0 = hf0

        def scan_body(t, carry):
            hf, hb = carry
            tb = nc - 1 - t
            sf = st_s[t, :, 0:A_GROUP_W]
            st_s[t, :, 0:A_GROUP_W] = hf
            hf = hf * cd_s[t, 0:1, 0:A_GROUP_W] + sf
            sb = st_s[tb, :, A_GROUP_W:]
            st_s[tb, :, A_GROUP_W:] = hb
            hb = hb * cd_s[tb, 0:1, A_GROUP_W:] + sb
            return hf, hb
        hf_fin, hb_fin = lax.fori_loop(0, nc, scan_body, (hf0, hb0))
        if emit_states:
            st_refs[0][0, A_HPG * g:A_HPG * (g + 1)] = hf_fin.T.reshape(A_HPG, A_HEAD_DIM, A_D_STATE)
            st_refs[1][0, A_HPG * g:A_HPG * (g + 1)] = hb_fin.T.reshape(A_HPG, A_HEAD_DIM, A_D_STATE)

        d_skip = dexp_ref[:, zc]
        nw = nw_ref[:, zc]

        def out_body(c):
            acs = acs_s[_rows(c), :]
            acsT = acsT_s[c]
            dtT = dtT_s[c]
            xs = xs_s[_rows(c), :]
            cmat = c_s[_rows(c), :]
            cbm = cb_s[c]
            ms, xblk = [], []
            for r in range(A_HPG):
                hf = A_HPG * g + r
                hb = A_HEADS + hf
                seg = jnp.where(ii >= jj, acs[:, hf:hf + 1] - acsT[hf:hf + 1, :],
                                acs[:, hb:hb + 1] - acsT[hb:hb + 1, :])
                dtf = dtT[hf:hf + 1, :]
                dtb = dtT[hb:hb + 1, :]
                wd = jnp.where(ii > jj, dtf, jnp.where(ii < jj, dtb, dtf + dtb))
                ms.append((cbm * jnp.exp(seg) * wd).astype(BF16))
                xblk.append(jnp.where(lane_g == r, xs, 0.0).astype(BF16))
            y = d_skip * xs + _dot(jnp.concatenate(ms, axis=1), jnp.concatenate(xblk, axis=0))
            e_in = _expand(jnp.exp(jnp.minimum(acs, 0.0)), e_fb)
            off = e_in * _dot(cmat, st_s[c].astype(BF16))
            y = y + off[:, 0:A_GROUP_W] + off[:, A_GROUP_W:]
            pz = proj_s[buf, pl.ds(pl.multiple_of(c * CHUNK + PROJ_HALO, 8), CHUNK), 0:A_GROUP_W]
            y = y * jax.nn.silu(pz)
            y = y * lax.rsqrt(jnp.mean(y * y, axis=-1, keepdims=True) + RMS_EPS) * nw
            ycat_s[_rows(c), zc] = y.astype(BF16)
        chunk_loop(out_body, unroll=HOST_UNROLL, host=(1 - buf, nxt))

    def ssd_all(*scratch):
        for g in range(A_GROUPS):
            ssd_group(g, *scratch)
    pl.run_scoped(
        ssd_all,
        pltpu.VMEM((L, A_GROUP_W), F32),
        pltpu.VMEM((nc, CHUNK, CHUNK), F32),
        pltpu.VMEM((L, A_D_STATE), BF16),
        pltpu.VMEM((nc, A_D_STATE, 2 * A_GROUP_W), F32),
        pltpu.VMEM((nc, 16, 2 * A_GROUP_W), F32),
    )

    def conformer(ubuf_s, uc_s):
        ubuf_s[pl.ds(0, 16), :] = jnp.zeros((16, B_BLK), F32)
        ubuf_s[pl.ds(16 + L, 16), :] = jnp.zeros((16, B_BLK), F32)
        for kb in range(B_BLOCKS):
            stage = A_GROUPS + kb
            buf = stage % 2
            nxt = stage_pieces[stage + 1] if kb + 1 < B_BLOCKS else []

            def glu_body(c, carry):
                src = pl.ds(pl.multiple_of(c * CHUNK + PROJ_HALO, 8), CHUNK)
                ubuf_s[pl.ds(pl.multiple_of(16 + c * CHUNK, 8), CHUNK), :] = (
                    proj_s[buf, src, 0:B_BLK] * jax.nn.sigmoid(proj_s[buf, src, B_BLK:2 * B_BLK]))
                return carry
            lax.fori_loop(0, nc, glu_body, 0, unroll=2)

            def conv_body(c):
                for lt in range(B_BLK // 128):
                    lanes = slice(128 * lt, 128 * (lt + 1))
                    wl = slice(B_BLK * kb + 128 * lt, B_BLK * kb + 128 * (lt + 1))
                    win = ubuf_s[pl.ds(pl.multiple_of(c * CHUNK, CHUNK), CHUNK + 32), lanes]
                    acc = jnp.broadcast_to(bcb_ref[:, wl], (CHUNK, 128))
                    for r in range(8):
                        sh = win if r == 0 else pltpu.roll(win, CHUNK + 32 - r, axis=0)
                        for k in [k for k in range(B_CONV) if (k + 1) % 8 == r]:
                            q = (k + 1) // 8
                            acc = acc + sh[8 * q:8 * q + CHUNK] * bcw_ref[k:k + 1, wl]
                    uc_s[_rows(c), wl] = acc
            chunk_loop(conv_body, unroll=HOST_UNROLL, host=(1 - buf, nxt))

        g = blg_ref[...]
        b = blb_ref[...]

        def ln_body(c):
            sgate = _dot(h_s[_rows(c), :], win_ref[:, W0_B + 2 * B_CH:W0_B + 3 * B_CH])
            for j in range(CHUNK // ROW_TILE):
                t = c * (CHUNK // ROW_TILE) + j
                u = jax.nn.silu(_layer_norm_rows(uc_s[_tile(t), :], g, b))
                sg = jax.nn.silu(sgate[ROW_TILE * j:ROW_TILE * (j + 1)])
                ycat_s[_tile(t), A_INNER:A_INNER + B_CH] = (u * sg).astype(BF16)
        chunk_loop(ln_body)

    pl.run_scoped(
        conformer,
        pltpu.VMEM((L + 32, B_BLK), F32),
        pltpu.VMEM((L, B_CH), F32),
    )

    _out_phase(L, x_ref, pos_refs, mods_ref, ycat_s, wout_ref, lng_ref, lnb_ref, out_ref)


def _const_spec(shape):
    nd = len(shape)
    return pl.BlockSpec(shape, lambda s: (0,) * nd, pipeline_mode=pl.Buffered(1))


def _l0_call(x, mods, mod_row0, pos_tabs, h0, weights, emit_states):
    nb, L, _ = x.shape
    nc = L // CHUNK
    has_pos = pos_tabs is not None
    has_h0 = h0 is not None
    row_of = (lambda s: (s + mod_row0, 0, 0)) if mod_row0 else (lambda s: (0, 0, 0))
    in_specs = [pl.BlockSpec((1, L, D_MODEL), lambda s: (s, 0, 0)),
                pl.BlockSpec((1, 1, 3 * D_MODEL), row_of)]
    args = [x, mods]
    if has_pos:
        in_specs += [_const_spec(t.shape) for t in pos_tabs]
        args += list(pos_tabs)
    st_spec = pl.BlockSpec((1, A_HEADS, A_HEAD_DIM, A_D_STATE), lambda s: (s, 0, 0, 0))
    if has_h0:
        in_specs += [st_spec, st_spec]
        args += list(h0)
    in_specs += [_const_spec(w.shape) for w in weights]
    args += list(weights)
    out_specs = [pl.BlockSpec((1, L, D_MODEL), lambda s: (s, 0, 0))]
    out_shape = [jax.ShapeDtypeStruct((nb, L, D_MODEL), F32)]
    if emit_states:
        out_specs += [st_spec, st_spec]
        out_shape += [jax.ShapeDtypeStruct((nb, A_HEADS, A_HEAD_DIM, A_D_STATE), F32)] * 2
    scratch = [pltpu.VMEM((L, D_MODEL), BF16),
               pltpu.VMEM((L, 2 * D_MODEL), BF16),
               pltpu.VMEM((L, HS_LANES), F32),
               pltpu.VMEM((L, HS_LANES), F32),
               pltpu.VMEM((nc, HS_LANES, CHUNK), F32),
               pltpu.VMEM((nc, HS_LANES, CHUNK), F32),
               pltpu.VMEM((2, L + 2 * PROJ_HALO, PROJ_W), F32)]
    return pl.pallas_call(
        functools.partial(_l0_kernel, L, has_pos, has_h0, emit_states),
        grid=(nb,),
        in_specs=in_specs,
        out_specs=out_specs,
        out_shape=out_shape,
        scratch_shapes=scratch,
        compiler_params=pltpu.CompilerParams(dimension_semantics=("arbitrary",), vmem_limit_bytes=VMEM_LIMIT),
        name="layer0_L%d" % L,
    )(*args)


def _l1_kernel(L, *refs):
    nc = L // CHUNK
    (x_ref, mods_ref, win_ref, cg_ref, cb_ref, ws_ref, bs_ref,
     wout_ref, lng_ref, lnb_ref, out_ref, h_s, ycat_s, vn_s) = refs
    shift = mods_ref[0, :, 0:D_MODEL]
    scale1 = 1.0 + mods_ref[0, :, D_MODEL:2 * D_MODEL]

    def mod_body(c, carry):
        _modulate_chunk(c, x_ref, None, shift, scale1, h_s)
        return carry
    lax.fori_loop(0, nc, mod_body, 0, unroll=2)

    def v_phase(pv_s):
        g = cg_ref[...]
        b = cb_ref[...]
        rb = min(L, OUT_BLOCK)
        nblk = L // rb
        half = ROW_TILE // 2

        def project(bi):
            pv_s[bi % 2] = _dot(h_s[pl.ds(bi * rb, rb), :], win_ref[:, C_WIDTH:2 * C_WIDTH])
        project(0)
        for bi in range(nblk):
            if bi + 1 < nblk:
                project(bi + 1)
            for j in range(rb // half):
                vn_s[pl.ds(bi * rb + j * half, half), :] = (
                    _layer_norm_rows(pv_s[bi % 2, pl.ds(j * half, half), :], g, b).astype(BF16))
    pl.run_scoped(v_phase, pltpu.VMEM((2, min(L, OUT_BLOCK), C_WIDTH), F32))

    def gate_phase(pug_s):
        cblk = 2 * C_GROUP_DIM
        nkb = C_WIDTH // cblk
        u = nc

        def project(kb, buf, row0, rows):
            hrows = h_s[pl.ds(row0, rows), :]
            pug_s[buf, pl.ds(row0, rows), 0:cblk] = _dot(hrows, win_ref[:, cblk * kb:cblk * (kb + 1)])
            pug_s[buf, pl.ds(row0, rows), cblk:] = _dot(
                hrows, win_ref[:, 2 * C_WIDTH + cblk * kb:2 * C_WIDTH + cblk * (kb + 1)])

        project(0, 0, 0, L)
        for kb in range(nkb):
            buf = kb % 2
            cols = slice(cblk * kb, cblk * (kb + 1))

            def body(c):
                parts = []
                for q in range(2):
                    grp = 2 * kb + q
                    vg = vn_s[_rows(c), C_GROUP_DIM * grp:C_GROUP_DIM * (grp + 1)]
                    parts.append(_dot(ws_ref[grp], vg))
                s = jnp.concatenate(parts, axis=1) + bs_ref[:, cols]
                pu = pug_s[buf, _rows(c), 0:cblk]
                pg = pug_s[buf, _rows(c), cblk:]
                ycat_s[_rows(c), cols] = (pu * s * jax.nn.silu(pg)).astype(BF16)

            def step(i, carry):
                if kb + 1 < nkb:
                    project(kb + 1, 1 - buf, pl.multiple_of(i * (u * CHUNK), u * CHUNK), u * CHUNK)
                for j in range(u):
                    body(i * u + j)
                return carry
            lax.fori_loop(0, nc // u, step, 0)
    pl.run_scoped(gate_phase, pltpu.VMEM((2, L, 4 * C_GROUP_DIM), F32))

    _out_phase(L, x_ref, None, mods_ref, ycat_s, wout_ref, lng_ref, lnb_ref, out_ref)


def _l1_call(x, mods, mod_row0, weights):
    nb, L, _ = x.shape
    row_of = (lambda s: (s + mod_row0, 0, 0)) if mod_row0 else (lambda s: (0, 0, 0))
    in_specs = [pl.BlockSpec((1, L, D_MODEL), lambda s: (s, 0, 0)),
                pl.BlockSpec((1, 1, 3 * D_MODEL), row_of)]
    in_specs += [_const_spec(w.shape) for w in weights]
    scratch = [pltpu.VMEM((L, D_MODEL), BF16),
               pltpu.VMEM((L, C_WIDTH), BF16),
               pltpu.VMEM((L, C_WIDTH), BF16)]
    return pl.pallas_call(
        functools.partial(_l1_kernel, L),
        grid=(nb,),
        in_specs=in_specs,
        out_specs=pl.BlockSpec((1, L, D_MODEL), lambda s: (s, 0, 0)),
        out_shape=jax.ShapeDtypeStruct((nb, L, D_MODEL), F32),
        scratch_shapes=scratch,
        compiler_params=pltpu.CompilerParams(dimension_semantics=("arbitrary",), vmem_limit_bytes=VMEM_LIMIT),
        name="layer1_L%d" % L,
    )(x, mods, *weights)


def _pos_tables(L):
    rows = L // GRID_W
    quarter = D_MODEL // 4
    freqs = 1.0 / (POS_BASE ** (jnp.arange(quarter, dtype=F32) / quarter))
    r = jnp.arange(rows, dtype=F32)[:, None] * freqs
    cc = jnp.arange(GRID_W, dtype=F32)[:, None] * freqs
    emb_r = jnp.concatenate([jnp.sin(r), jnp.cos(r)], axis=-1)
    emb_c = jnp.concatenate([jnp.sin(cc), jnp.cos(cc)], axis=-1)
    return jnp.broadcast_to(emb_r[:, None, :], (rows, 8, D_MODEL // 2)), emb_c


def _hs_row(f, b):
    return jnp.pad(jnp.concatenate([f, b]).astype(F32), (0, HS_LANES - 2 * A_HEADS)).reshape(1, HS_LANES)


def kernel(x_prompt, x_sample, state_ssd_fwd_l0, state_ssd_bwd_l0, c, c_ctx, ada_w_l0, ada_b_l0, w_in_l0, a_conv_w_l0, a_conv_b_l0, a_dt_bias_f_l0, a_dt_bias_b_l0, a_log_f_l0, a_log_b_l0, a_d_l0, a_norm_w_l0, b_conv_w_l0, b_conv_b_l0, b_ln_g_l0, b_ln_b_l0, w_out_l0, ln_g_l0, ln_b_l0, ada_w_l1, ada_b_l1, w_in_l1, c_ln_g_l1, c_ln_b_l1, c_ws_l1, c_bs_l1, w_out_l1, ln_g_l1, ln_b_l1):
    dec_batch = c.shape[0]
    cond = jnp.concatenate([c_ctx[None, :], c, jnp.zeros((ADA_ROWS - 1 - dec_batch, D_MODEL), F32)], axis=0)
    mods0, mods1 = _ada_call(cond, ada_w_l0, ada_b_l0, ada_w_l1, ada_b_l1)
    mods0 = mods0.reshape(ADA_ROWS, 1, 3 * D_MODEL)
    mods1 = mods1.reshape(ADA_ROWS, 1, 3 * D_MODEL)

    a_in = A_INNER + A_INNER + 2 * A_GN
    row = lambda v: v.astype(F32).reshape(1, -1)
    w0 = [
        jnp.concatenate([w_in_l0[:, :a_in],
                         jnp.pad(w_in_l0[:, a_in:a_in + 2 * A_HEADS], ((0, 0), (0, HS_LANES - 2 * A_HEADS))),
                         w_in_l0[:, a_in + 2 * A_HEADS:]], axis=1).astype(BF16),
        a_conv_w_l0.astype(F32), row(a_conv_b_l0),
        _hs_row(a_dt_bias_f_l0, a_dt_bias_b_l0), _hs_row(a_log_f_l0, a_log_b_l0),
        row(jnp.repeat(a_d_l0, A_HEAD_DIM)), row(a_norm_w_l0),
        jnp.pad(b_conv_w_l0.astype(F32), ((0, 1), (0, 0))), row(b_conv_b_l0), row(b_ln_g_l0), row(b_ln_b_l0),
        w_out_l0.astype(BF16), row(ln_g_l0), row(ln_b_l0),
    ]
    pos_tabs = _pos_tables(x_sample.shape[1])
    xp1, hf, hb = _l0_call(x_prompt, mods0, 0, None, None, w0, True)
    (xs1,) = _l0_call(x_sample, mods0, 1, pos_tabs, (state_ssd_fwd_l0, state_ssd_bwd_l0), w0, False)

    w1 = [
        w_in_l1.astype(BF16),
        row(c_ln_g_l1), row(c_ln_b_l1), c_ws_l1.astype(BF16),
        jnp.repeat(c_bs_l1.astype(F32).T, C_GROUP_DIM, axis=1),
        w_out_l1.astype(BF16), row(ln_g_l1), row(ln_b_l1),
    ]
    blk = x_sample.shape[1]
    if xp1.size % (blk * D_MODEL) == 0 and blk % x_prompt.shape[1] == 0:
        y_prompt = _l1_call(xp1.reshape(-1, blk, D_MODEL), mods1, 0, w1).reshape(xp1.shape)
    else:
        y_prompt = _l1_call(xp1, mods1, 0, w1)
    y_sample = _l1_call(xs1, mods1, 1, w1)
    return (y_prompt, y_sample, hf, hb)
```

```python
import functools

import jax
import jax.numpy as jnp
from jax import lax
from jax.experimental import pallas as pl
from jax.experimental.pallas import tpu as pltpu

F32 = jnp.float32
BF16 = jnp.bfloat16
HIGHEST = lax.Precision.HIGHEST

D_MODEL = 1024
GRID_W = 64
POS_BASE = 10000.0
A_INNER = 1024
A_HEAD_DIM = 64
A_HEADS = 16
A_GROUPS = 4
A_HPG = 4
A_D_STATE = 128
A_GN = 512
A_CONV = 4
A_GROUP_W = A_HPG * A_HEAD_DIM
B_CH = 1024
B_CONV = 31
B_BLK = 256
B_BLOCKS = B_CH // B_BLK
C_WIDTH = 2048
C_GROUPS = 16
C_GROUP_DIM = 128
CHUNK = 128
ROW_TILE = 32
OUT_BLOCK = 512
HOST_UNROLL = 4
PROJ_W = 3 * B_BLK
PROJ_HALO = 8
DEPTH = 2
ALPHA = (2 * DEPTH) ** 0.25
LN_EPS = 1e-5
RMS_EPS = 1e-5
HS_LANES = 128
W0_DT = 2 * A_INNER + 2 * A_GN
W0_B = W0_DT + HS_LANES
ADA_ROWS = 16
VMEM_LIMIT = 58 * 1024 * 1024


def _dot(a, b):
    return jnp.dot(a, b, preferred_element_type=F32)


def _dot_nt(a, b):
    return lax.dot_general(a, b, (((1,), (1,)), ((), ())), preferred_element_type=F32)


def _layer_norm_rows(t, g, b):
    mu = jnp.mean(t, axis=-1, keepdims=True)
    tc = t - mu
    var = jnp.mean(tc * tc, axis=-1, keepdims=True)
    return tc * lax.rsqrt(var + LN_EPS) * g + b


def _expand(v, e):
    hi = v.astype(BF16)
    lo = (v - hi.astype(F32)).astype(BF16)
    return _dot(hi, e) + _dot(lo, e)


def _rows(c):
    if isinstance(c, int):
        return pl.ds(c * CHUNK, CHUNK)
    return pl.ds(pl.multiple_of(c * CHUNK, CHUNK), CHUNK)


def _tile(t):
    if isinstance(t, int):
        return pl.ds(t * ROW_TILE, ROW_TILE)
    return pl.ds(pl.multiple_of(t * ROW_TILE, ROW_TILE), ROW_TILE)


def _ada_kernel(cond_ref, w0_ref, b0_ref, w1_ref, b1_ref, o0_ref, o1_ref):
    s = jax.nn.silu(cond_ref[...])
    o0_ref[...] = jnp.dot(s, w0_ref[...], preferred_element_type=F32, precision=HIGHEST) + b0_ref[...]
    o1_ref[...] = jnp.dot(s, w1_ref[...], preferred_element_type=F32, precision=HIGHEST) + b1_ref[...]


def _ada_call(cond, w0, b0, w1, b1):
    bn = 768
    n = 3 * D_MODEL
    wspec = pl.BlockSpec((D_MODEL, bn), lambda j: (0, j))
    bspec = pl.BlockSpec((1, bn), lambda j: (0, j))
    ospec = pl.BlockSpec((ADA_ROWS, bn), lambda j: (0, j))
    return pl.pallas_call(
        _ada_kernel,
        grid=(n // bn,),
        in_specs=[pl.BlockSpec((ADA_ROWS, D_MODEL), lambda j: (0, 0)), wspec, bspec, wspec, bspec],
        out_specs=[ospec, ospec],
        out_shape=[jax.ShapeDtypeStruct((ADA_ROWS, n), F32)] * 2,
        compiler_params=pltpu.CompilerParams(dimension_semantics=("arbitrary",)),
        name="adaln",
    )(cond, w0, b0.reshape(1, n), w1, b1.reshape(1, n))


def _x_tile(x_ref, pos_refs, t):
    xt = x_ref[0, _tile(t), :]
    if pos_refs is not None:
        er_ref, ec_ref = pos_refs
        tiles_per_grid_row = GRID_W // ROW_TILE
        if isinstance(t, int):
            grid_row, col0 = t // tiles_per_grid_row, (t % tiles_per_grid_row) * ROW_TILE
        else:
            grid_row = lax.div(t, tiles_per_grid_row)
            col0 = pl.multiple_of(lax.rem(t, tiles_per_grid_row) * ROW_TILE, ROW_TILE)
        left = jnp.tile(er_ref[grid_row], (ROW_TILE // 8, 1))
        xt = xt + jnp.concatenate([left, ec_ref[pl.ds(col0, ROW_TILE), :]], axis=1)
    return xt


def _modulate_chunk(c, x_ref, pos_refs, shift, scale1, h_s):
    for j in range(CHUNK // ROW_TILE):
        t = c * (CHUNK // ROW_TILE) + j
        h_s[_tile(t), :] = (_x_tile(x_ref, pos_refs, t) * scale1 + shift).astype(BF16)


def _out_phase(L, x_ref, pos_refs, mods_ref, ycat_s, wout_ref, lng_ref, lnb_ref, out_ref, k0=0):
    gate = mods_ref[0, :, 2 * D_MODEL:3 * D_MODEL]
    g = lng_ref[...]
    b = lnb_ref[...]
    rb = min(L, OUT_BLOCK)
    nblk = L // rb
    tiles = rb // ROW_TILE

    def run(y_s):
        def project(bi):
            y_s[bi % 2] = _dot(ycat_s[pl.ds(bi * rb, rb), k0:], wout_ref[k0:, :])
        project(0)
        for bi in range(nblk):
            if bi + 1 < nblk:
                project(bi + 1)
            for j in range(tiles):
                tt = bi * tiles + j
                y = y_s[bi % 2, _tile(j), :]
                if k0:
                    y = y + out_ref[0, _tile(tt), :]
                v = ALPHA * _x_tile(x_ref, pos_refs, tt) + gate * y
                out_ref[0, _tile(tt), :] = _layer_norm_rows(v, g, b)
    pl.run_scoped(run, pltpu.VMEM((2, rb, D_MODEL), F32))


def _ssd_pieces(w_ref, g):
    return [(w_ref, A_GROUP_W * g, 0, A_GROUP_W),
            (w_ref, A_INNER + A_GROUP_W * g, A_GROUP_W, A_GROUP_W),
            (w_ref, 2 * A_INNER + A_D_STATE * g, 2 * A_GROUP_W, A_D_STATE),
            (w_ref, 2 * A_INNER + A_GN + A_D_STATE * g, 2 * A_GROUP_W + A_D_STATE, A_D_STATE)]


def _conf_pieces(w_ref, kb):
    return [(w_ref, W0_B + B_BLK * kb, 0, B_BLK),
            (w_ref, W0_B + B_CH + B_BLK * kb, B_BLK, B_BLK)]


def _l0_kernel(L, has_pos, has_h0, emit_states, *refs):
    nc = L // CHUNK
    it = iter(refs)
    x_ref = next(it)
    mods_ref = next(it)
    pos_refs = (next(it), next(it)) if has_pos else None
    h0_refs = (next(it), next(it)) if has_h0 else None
    (win_ref, acw_ref, acb_ref, dtb_ref, alog_ref, dexp_ref, nw_ref,
     bcw_ref, bcb_ref, blg_ref, blb_ref, wout_ref, lng_ref, lnb_ref) = [next(it) for _ in range(14)]
    out_ref = next(it)
    st_refs = (next(it), next(it)) if emit_states else None
    h_s, ycat_s, dt_s, acs_s, dtT_s, acsT_s, proj_s = [next(it) for _ in range(7)]

    def project_rows(buf, pieces, row0, rows):
        if pieces:
            hrows = h_s[pl.ds(row0, rows), :]
            dst0 = row0 + PROJ_HALO if isinstance(row0, int) else pl.multiple_of(row0 + PROJ_HALO, 8)
            for w_ref, wcol, pcol, width in pieces:
                proj_s[buf, pl.ds(dst0, rows), pcol:pcol + width] = _dot(hrows, w_ref[:, wcol:wcol + width])

    def chunk_loop(body, unroll=2, host=None):
        u = min(nc, unroll)

        def step(i, carry):
            if host is not None:
                project_rows(host[0], host[1], pl.multiple_of(i * (u * CHUNK), u * CHUNK), u * CHUNK)
            for j in range(u):
                body(i * u + j)
            return carry
        lax.fori_loop(0, nc // u, step, 0)

    stage_pieces = ([_ssd_pieces(win_ref, g) for g in range(A_GROUPS)]
                    + [_conf_pieces(win_ref, kb) for kb in range(B_BLOCKS)])

    for buf in range(2):
        proj_s[buf, pl.ds(0, PROJ_HALO), :] = jnp.zeros((PROJ_HALO, PROJ_W), F32)
        proj_s[buf, pl.ds(PROJ_HALO + L, PROJ_HALO), :] = jnp.zeros((PROJ_HALO, PROJ_W), F32)

    shift = mods_ref[0, :, 0:D_MODEL]
    scale1 = 1.0 + mods_ref[0, :, D_MODEL:2 * D_MODEL]

    def head_body(c, carry):
        _modulate_chunk(c, x_ref, pos_refs, shift, scale1, h_s)
        return carry
    lax.fori_loop(0, nc, head_body, 0, unroll=2)
    dt_s[...] = _dot(h_s[...], win_ref[:, W0_DT:W0_DT + HS_LANES])

    ii = lax.broadcasted_iota(jnp.int32, (CHUNK, CHUNK), 0)
    jj = lax.broadcasted_iota(jnp.int32, (CHUNK, CHUNK), 1)
    lane_hs = lax.broadcasted_iota(jnp.int32, (1, HS_LANES), 1)
    is_fwd_lane = lane_hs < A_HEADS

    tril = (ii >= jj).astype(F32)
    triu = (ii <= jj).astype(F32)
    a_neg = -jnp.exp(alog_ref[...])

    def dt_body(c):
        dt = jax.nn.softplus(dt_s[_rows(c), :] + dtb_ref[...])
        dt_s[_rows(c), :] = dt
        adt = dt * a_neg
        cum_f = jnp.dot(tril, adt, preferred_element_type=F32, precision=HIGHEST)
        cum_b = jnp.dot(triu, adt, preferred_element_type=F32, precision=HIGHEST)
        acs = jnp.where(is_fwd_lane, cum_f, cum_b)
        acs_s[_rows(c), :] = acs
        acsT_s[c] = acs.T
        dtT_s[c] = dt.T
    chunk_loop(dt_body, unroll=HOST_UNROLL, host=(0, stage_pieces[0]))

    lane_g = lax.broadcasted_iota(jnp.int32, (CHUNK, A_GROUP_W), 1) // A_HEAD_DIM
    exp_row = lax.broadcasted_iota(jnp.int32, (HS_LANES, 2 * A_GROUP_W), 0)
    exp_col = lax.broadcasted_iota(jnp.int32, (HS_LANES, 2 * A_GROUP_W), 1)
    exp_src = exp_col // A_HEAD_DIM + jnp.where(exp_col < A_GROUP_W, 0, A_HEADS - A_HPG)

    def ssd_group(g, xs_s, cb_s, c_s, st_s, cd_s):
        buf = g % 2
        nxt = stage_pieces[g + 1]
        e_fb = (exp_row == A_HPG * g + exp_src).astype(BF16)
        zc = slice(A_GROUP_W * g, A_GROUP_W * (g + 1))

        def conv_cols(ref):
            x0 = A_GROUP_W * g
            b0 = A_INNER + A_D_STATE * g
            c0 = A_INNER + A_GN + A_D_STATE * g
            return jnp.concatenate([ref[:, x0:x0 + A_GROUP_W], ref[:, b0:b0 + A_D_STATE],
                                    ref[:, c0:c0 + A_D_STATE]], axis=1)
        cw = conv_cols(acw_ref)
        cbias = conv_cols(acb_ref)

        def chunk_scalars(c):
            acs = acs_s[_rows(c), :]
            dt = dt_s[_rows(c), :]
            ref_row = jnp.where(is_fwd_lane, acs[CHUNK - 1:CHUNK, :], acs[0:1, :])
            decay_end = jnp.exp(jnp.minimum(ref_row - acs, 0.0))
            chunk_decay = jnp.broadcast_to(jnp.exp(jnp.minimum(ref_row, 0.0)), (16, HS_LANES))
            return decay_end * dt, chunk_decay

        def conv_body(c):
            win = proj_s[buf, pl.ds(pl.multiple_of(c * CHUNK, CHUNK), CHUNK + 2 * PROJ_HALO), A_GROUP_W:]
            acc = cbias
            for k in range(A_CONV):
                d = 6 + k - PROJ_HALO
                sh = win if d == 0 else pltpu.roll(win, (-d) % (CHUNK + 2 * PROJ_HALO), axis=0)
                acc = acc + sh[PROJ_HALO:PROJ_HALO + CHUNK] * cw[k:k + 1]
            act = jax.nn.silu(acc)
            xs = act[:, 0:A_GROUP_W]
            bm = act[:, A_GROUP_W:A_GROUP_W + A_D_STATE]
            xs_s[_rows(c), :] = xs
            cm = act[:, A_GROUP_W + A_D_STATE:].astype(BF16)
            c_s[_rows(c), :] = cm
            cb_s[c] = _dot_nt(cm, bm.astype(BF16))
            w, chunk_decay = chunk_scalars(c)
            wexp = _expand(w, e_fb)
            xw = jnp.concatenate([xs * wexp[:, 0:A_GROUP_W], xs * wexp[:, A_GROUP_W:]], axis=1)
            st_s[c] = _dot(bm.T.astype(BF16), xw.astype(BF16))
            cd_s[c] = _expand(chunk_decay, e_fb)
        chunk_loop(conv_body, unroll=4)

        if has_h0:
            hf0 = h0_refs[0][0, A_HPG * g:A_HPG * (g + 1)].reshape(A_GROUP_W, A_D_STATE).T
            hb0 = h0_refs[1][0, A_HPG * g:A_HPG * (g + 1)].reshape(A_GROUP_W, A_D_STATE).T
        else:
            hf0 = jnp.zeros((A_D_STATE, A_GROUP_W), F32)
            hb0 = hf0

        def scan_body(t, carry):
            hf, hb = carry
            tb = nc - 1 - t
            sf = st_s[t, :, 0:A_GROUP_W]
            st_s[t, :, 0:A_GROUP_W] = hf
            hf = hf * cd_s[t, 0:1, 0:A_GROUP_W] + sf
            sb = st_s[tb, :, A_GROUP_W:]
            st_s[tb, :, A_GROUP_W:] = hb
            hb = hb * cd_s[tb, 0:1, A_GROUP_W:] + sb
            return hf, hb
        hf_fin, hb_fin = lax.fori_loop(0, nc, scan_body, (hf0, hb0), unroll=True)
        if emit_states:
            st_refs[0][0, A_HPG * g:A_HPG * (g + 1)] = hf_fin.T.reshape(A_HPG, A_HEAD_DIM, A_D_STATE)
            st_refs[1][0, A_HPG * g:A_HPG * (g + 1)] = hb_fin.T.reshape(A_HPG, A_HEAD_DIM, A_D_STATE)

        d_skip = dexp_ref[:, zc]
        nw = nw_ref[:, zc]

        def out_body(c):
            acs = acs_s[_rows(c), :]
            acsT = acsT_s[c]
            dtT = dtT_s[c]
            xs = xs_s[_rows(c), :]
            cmat = c_s[_rows(c), :]
            cbm = cb_s[c]
            ms, xblk = [], []
            for r in range(A_HPG):
                hf = A_HPG * g + r
                hb = A_HEADS + hf
                seg = jnp.where(ii >= jj, acs[:, hf:hf + 1] - acsT[hf:hf + 1, :],
                                acs[:, hb:hb + 1] - acsT[hb:hb + 1, :])
                dtf = dtT[hf:hf + 1, :]
                dtb = dtT[hb:hb + 1, :]
                wd = jnp.where(ii > jj, dtf, jnp.where(ii < jj, dtb, dtf + dtb))
                ms.append((cbm * jnp.exp(seg) * wd).astype(BF16))
                xblk.append(jnp.where(lane_g == r, xs, 0.0).astype(BF16))
            y = d_skip * xs + _dot(jnp.concatenate(ms, axis=1), jnp.concatenate(xblk, axis=0))
            e_in = _expand(jnp.exp(jnp.minimum(acs, 0.0)), e_fb)
            off = e_in * _dot(cmat, st_s[c].astype(BF16))
            y = y + off[:, 0:A_GROUP_W] + off[:, A_GROUP_W:]
            pz = proj_s[buf, pl.ds(pl.multiple_of(c * CHUNK + PROJ_HALO, 8), CHUNK), 0:A_GROUP_W]
            y = y * jax.nn.silu(pz)
            y = y * lax.rsqrt(jnp.mean(y * y, axis=-1, keepdims=True) + RMS_EPS) * nw
            ycat_s[_rows(c), zc] = y.astype(BF16)
        chunk_loop(out_body, unroll=HOST_UNROLL, host=(1 - buf, nxt))

    def ssd_all(*scratch):
        for g in range(A_GROUPS):
            ssd_group(g, *scratch)
    pl.run_scoped(
        ssd_all,
        pltpu.VMEM((L, A_GROUP_W), F32),
        pltpu.VMEM((nc, CHUNK, CHUNK), F32),
        pltpu.VMEM((L, A_D_STATE), BF16),
        pltpu.VMEM((nc, A_D_STATE, 2 * A_GROUP_W), F32),
        pltpu.VMEM((nc, 16, 2 * A_GROUP_W), F32),
    )

    def conformer(ubuf_s, uc_s):
        ubuf_s[pl.ds(0, 16), :] = jnp.zeros((16, B_BLK), F32)
        ubuf_s[pl.ds(16 + L, 16), :] = jnp.zeros((16, B_BLK), F32)
        for kb in range(B_BLOCKS):
            stage = A_GROUPS + kb
            buf = stage % 2
            nxt = stage_pieces[stage + 1] if kb + 1 < B_BLOCKS else []

            def glu_body(c, carry):
                src = pl.ds(pl.multiple_of(c * CHUNK + PROJ_HALO, 8), CHUNK)
                ubuf_s[pl.ds(pl.multiple_of(16 + c * CHUNK, 8), CHUNK), :] = (
                    proj_s[buf, src, 0:B_BLK] * jax.nn.sigmoid(proj_s[buf, src, B_BLK:2 * B_BLK]))
                return carry
            lax.fori_loop(0, nc, glu_body, 0, unroll=2)

            def conv_body(c):
                for lt in range(B_BLK // 128):
                    lanes = slice(128 * lt, 128 * (lt + 1))
                    wl = slice(B_BLK * kb + 128 * lt, B_BLK * kb + 128 * (lt + 1))
                    win = ubuf_s[pl.ds(pl.multiple_of(c * CHUNK, CHUNK), CHUNK + 32), lanes]
                    acc = jnp.broadcast_to(bcb_ref[:, wl], (CHUNK, 128))
                    for r in range(8):
                        sh = win if r == 0 else pltpu.roll(win, CHUNK + 32 - r, axis=0)
                        for k in [k for k in range(B_CONV) if (k + 1) % 8 == r]:
                            q = (k + 1) // 8
                            acc = acc + sh[8 * q:8 * q + CHUNK] * bcw_ref[k:k + 1, wl]
                    uc_s[_rows(c), wl] = acc
            chunk_loop(conv_body, unroll=HOST_UNROLL, host=(1 - buf, nxt))

        g = blg_ref[...]
        b = blb_ref[...]

        def ln_body(c):
            sgate = _dot(h_s[_rows(c), :], win_ref[:, W0_B + 2 * B_CH:W0_B + 3 * B_CH])
            for j in range(CHUNK // ROW_TILE):
                t = c * (CHUNK // ROW_TILE) + j
                u = jax.nn.silu(_layer_norm_rows(uc_s[_tile(t), :], g, b))
                sg = jax.nn.silu(sgate[ROW_TILE * j:ROW_TILE * (j + 1)])
                ycat_s[_tile(t), A_INNER:A_INNER + B_CH] = (u * sg).astype(BF16)
        chunk_loop(ln_body)

    pl.run_scoped(
        conformer,
        pltpu.VMEM((L + 32, B_BLK), F32),
        pltpu.VMEM((L, B_CH), F32),
    )

    _out_phase(L, x_ref, pos_refs, mods_ref, ycat_s, wout_ref, lng_ref, lnb_ref, out_ref)


def _const_spec(shape):
    nd = len(shape)
    return pl.BlockSpec(shape, lambda s: (0,) * nd, pipeline_mode=pl.Buffered(1))


def _l0_call(x, mods, mod_row0, pos_tabs, h0, weights, emit_states):
    nb, L, _ = x.shape
    nc = L // CHUNK
    has_pos = pos_tabs is not None
    has_h0 = h0 is not None
    row_of = (lambda s: (s + mod_row0, 0, 0)) if mod_row0 else (lambda s: (0, 0, 0))
    in_specs = [pl.BlockSpec((1, L, D_MODEL), lambda s: (s, 0, 0)),
                pl.BlockSpec((1, 1, 3 * D_MODEL), row_of)]
    args = [x, mods]
    if has_pos:
        in_specs += [_const_spec(t.shape) for t in pos_tabs]
        args += list(pos_tabs)
    st_spec = pl.BlockSpec((1, A_HEADS, A_HEAD_DIM, A_D_STATE), lambda s: (s, 0, 0, 0))
    if has_h0:
        in_specs += [st_spec, st_spec]
        args += list(h0)
    in_specs += [_const_spec(w.shape) for w in weights]
    args += list(weights)
    out_specs = [pl.BlockSpec((1, L, D_MODEL), lambda s: (s, 0, 0))]
    out_shape = [jax.ShapeDtypeStruct((nb, L, D_MODEL), F32)]
    if emit_states:
        out_specs += [st_spec, st_spec]
        out_shape += [jax.ShapeDtypeStruct((nb, A_HEADS, A_HEAD_DIM, A_D_STATE), F32)] * 2
    scratch = [pltpu.VMEM((L, D_MODEL), BF16),
               pltpu.VMEM((L, 2 * D_MODEL), BF16),
               pltpu.VMEM((L, HS_LANES), F32),
               pltpu.VMEM((L, HS_LANES), F32),
               pltpu.VMEM((nc, HS_LANES, CHUNK), F32),
               pltpu.VMEM((nc, HS_LANES, CHUNK), F32),
               pltpu.VMEM((2, L + 2 * PROJ_HALO, PROJ_W), F32)]
    return pl.pallas_call(
        functools.partial(_l0_kernel, L, has_pos, has_h0, emit_states),
        grid=(nb,),
        in_specs=in_specs,
        out_specs=out_specs,
        out_shape=out_shape,
        scratch_shapes=scratch,
        compiler_params=pltpu.CompilerParams(dimension_semantics=("arbitrary",), vmem_limit_bytes=VMEM_LIMIT),
        name="layer0_L%d" % L,
    )(*args)


def _l1_kernel(L, *refs):
    nc = L // CHUNK
    (x_ref, mods_ref, win_ref, cg_ref, cb_ref, ws_ref, bs_ref,
     wout_ref, lng_ref, lnb_ref, out_ref, h_s, ycat_s, vn_s) = refs
    shift = mods_ref[0, :, 0:D_MODEL]
    scale1 = 1.0 + mods_ref[0, :, D_MODEL:2 * D_MODEL]

    def mod_body(c, carry):
        _modulate_chunk(c, x_ref, None, shift, scale1, h_s)
        return carry
    lax.fori_loop(0, nc, mod_body, 0, unroll=2)

    def v_phase(pv_s):
        g = cg_ref[...]
        b = cb_ref[...]
        rb = min(L, OUT_BLOCK)
        nblk = L // rb
        half = ROW_TILE // 2

        def project(bi):
            pv_s[bi % 2] = _dot(h_s[pl.ds(bi * rb, rb), :], win_ref[:, C_WIDTH:2 * C_WIDTH])
        project(0)
        for bi in range(nblk):
            if bi + 1 < nblk:
                project(bi + 1)
            for j in range(rb // half):
                vn_s[pl.ds(bi * rb + j * half, half), :] = (
                    _layer_norm_rows(pv_s[bi % 2, pl.ds(j * half, half), :], g, b).astype(BF16))
    pl.run_scoped(v_phase, pltpu.VMEM((2, min(L, OUT_BLOCK), C_WIDTH), F32))

    def gate_phase(pug_s):
        cblk = 2 * C_GROUP_DIM
        nkb = C_WIDTH // cblk
        u = nc

        def project(kb, buf, row0, rows):
            hrows = h_s[pl.ds(row0, rows), :]
            pug_s[buf, pl.ds(row0, rows), 0:cblk] = _dot(hrows, win_ref[:, cblk * kb:cblk * (kb + 1)])
            pug_s[buf, pl.ds(row0, rows), cblk:] = _dot(
                hrows, win_ref[:, 2 * C_WIDTH + cblk * kb:2 * C_WIDTH + cblk * (kb + 1)])

        project(0, 0, 0, L)
        for kb in range(nkb):
            buf = kb % 2
            cols = slice(cblk * kb, cblk * (kb + 1))

            def body(c):
                parts = []
                for q in range(2):
                    grp = 2 * kb + q
                    vg = vn_s[_rows(c), C_GROUP_DIM * grp:C_GROUP_DIM * (grp + 1)]
                    parts.append(_dot(ws_ref[grp], vg))
                s = jnp.concatenate(parts, axis=1) + bs_ref[:, cols]
                pu = pug_s[buf, _rows(c), 0:cblk]
                pg = pug_s[buf, _rows(c), cblk:]
                ycat_s[_rows(c), cols] = (pu * s * jax.nn.silu(pg)).astype(BF16)

            def step(i, carry):
                if kb + 1 < nkb:
                    project(kb + 1, 1 - buf, pl.multiple_of(i * (u * CHUNK), u * CHUNK), u * CHUNK)
                for j in range(u):
                    body(i * u + j)
                return carry
            lax.fori_loop(0, nc // u, step, 0)
    pl.run_scoped(gate_phase, pltpu.VMEM((2, L, 4 * C_GROUP_DIM), F32))

    _out_phase(L, x_ref, None, mods_ref, ycat_s, wout_ref, lng_ref, lnb_ref, out_ref)


def _l1_call(x, mods, mod_row0, weights):
    nb, L, _ = x.shape
    row_of = (lambda s: (s + mod_row0, 0, 0)) if mod_row0 else (lambda s: (0, 0, 0))
    in_specs = [pl.BlockSpec((1, L, D_MODEL), lambda s: (s, 0, 0)),
                pl.BlockSpec((1, 1, 3 * D_MODEL), row_of)]
    in_specs += [_const_spec(w.shape) for w in weights]
    scratch = [pltpu.VMEM((L, D_MODEL), BF16),
               pltpu.VMEM((L, C_WIDTH), BF16),
               pltpu.VMEM((L, C_WIDTH), BF16)]
    return pl.pallas_call(
        functools.partial(_l1_kernel, L),
        grid=(nb,),
        in_specs=in_specs,
        out_specs=pl.BlockSpec((1, L, D_MODEL), lambda s: (s, 0, 0)),
        out_shape=jax.ShapeDtypeStruct((nb, L, D_MODEL), F32),
        scratch_shapes=scratch,
        compiler_params=pltpu.CompilerParams(dimension_semantics=("arbitrary",), vmem_limit_bytes=VMEM_LIMIT),
        name="layer1_L%d" % L,
    )(x, mods, *weights)


def _pos_tables(L):
    rows = L // GRID_W
    quarter = D_MODEL // 4
    freqs = 1.0 / (POS_BASE ** (jnp.arange(quarter, dtype=F32) / quarter))
    r = jnp.arange(rows, dtype=F32)[:, None] * freqs
    cc = jnp.arange(GRID_W, dtype=F32)[:, None] * freqs
    emb_r = jnp.concatenate([jnp.sin(r), jnp.cos(r)], axis=-1)
    emb_c = jnp.concatenate([jnp.sin(cc), jnp.cos(cc)], axis=-1)
    return jnp.broadcast_to(emb_r[:, None, :], (rows, 8, D_MODEL // 2)), emb_c


def _hs_row(f, b):
    return jnp.pad(jnp.concatenate([f, b]).astype(F32), (0, HS_LANES - 2 * A_HEADS)).reshape(1, HS_LANES)


def kernel(x_prompt, x_sample, state_ssd_fwd_l0, state_ssd_bwd_l0, c, c_ctx, ada_w_l0, ada_b_l0, w_in_l0, a_conv_w_l0, a_conv_b_l0, a_dt_bias_f_l0, a_dt_bias_b_l0, a_log_f_l0, a_log_b_l0, a_d_l0, a_norm_w_l0, b_conv_w_l0, b_conv_b_l0, b_ln_g_l0, b_ln_b_l0, w_out_l0, ln_g_l0, ln_b_l0, ada_w_l1, ada_b_l1, w_in_l1, c_ln_g_l1, c_ln_b_l1, c_ws_l1, c_bs_l1, w_out_l1, ln_g_l1, ln_b_l1):
    dec_batch = c.shape[0]
    cond = jnp.concatenate([c_ctx[None, :], c, jnp.zeros((ADA_ROWS - 1 - dec_batch, D_MODEL), F32)], axis=0)
    mods0, mods1 = _ada_call(cond, ada_w_l0, ada_b_l0, ada_w_l1, ada_b_l1)
    mods0 = mods0.reshape(ADA_ROWS, 1, 3 * D_MODEL)
    mods1 = mods1.reshape(ADA_ROWS, 1, 3 * D_MODEL)

    a_in = A_INNER + A_INNER + 2 * A_GN
    row = lambda v: v.astype(F32).reshape(1, -1)
    w0 = [
        jnp.concatenate([w_in_l0[:, :a_in],
                         jnp.pad(w_in_l0[:, a_in:a_in + 2 * A_HEADS], ((0, 0), (0, HS_LANES - 2 * A_HEADS))),
                         w_in_l0[:, a_in + 2 * A_HEADS:]], axis=1).astype(BF16),
        a_conv_w_l0.astype(F32), row(a_conv_b_l0),
        _hs_row(a_dt_bias_f_l0, a_dt_bias_b_l0), _hs_row(a_log_f_l0, a_log_b_l0),
        row(jnp.repeat(a_d_l0, A_HEAD_DIM)), row(a_norm_w_l0),
        jnp.pad(b_conv_w_l0.astype(F32), ((0, 1), (0, 0))), row(b_conv_b_l0), row(b_ln_g_l0), row(b_ln_b_l0),
        w_out_l0.astype(BF16), row(ln_g_l0), row(ln_b_l0),
    ]
    pos_tabs = _pos_tables(x_sample.shape[1])
    xp1, hf, hb = _l0_call(x_prompt, mods0, 0, None, None, w0, True)
    (xs1,) = _l0_call(x_sample, mods0, 1, pos_tabs, (state_ssd_fwd_l0, state_ssd_bwd_l0), w0, False)

    w1 = [
        w_in_l1.astype(BF16),
        row(c_ln_g_l1), row(c_ln_b_l1), c_ws_l1.astype(BF16),
        jnp.repeat(c_bs_l1.astype(F32).T, C_GROUP_DIM, axis=1),
        w_out_l1.astype(BF16), row(ln_g_l1), row(ln_b_l1),
    ]
    blk = x_sample.shape[1]
    if xp1.size % (blk * D_MODEL) == 0 and blk % x_prompt.shape[1] == 0:
        y_prompt = _l1_call(xp1.reshape(-1, blk, D_MODEL), mods1, 0, w1).reshape(xp1.shape)
    else:
        y_prompt = _l1_call(xp1, mods1, 0, w1)
    y_sample = _l1_call(xs1, mods1, 1, w1)
    return (y_prompt, y_sample, hf, hb)
```
